```python
import math
import jax, jax.numpy as jnp
from jax import lax
import numpy as np

D_MODEL = 1024
BATCH = 4
SEQ = 8192
DEPTH = 2

GRID_W = 64
CTX_LEN = 256

N_MIXERS = 2
N_SSD_LAYERS = (DEPTH + 1) // 2
N_ATTN_LAYERS = DEPTH // 2

SSD_EXPAND = 2
SSD_D_INNER = SSD_EXPAND * D_MODEL
SSD_HEAD_DIM = 64
SSD_HEADS = SSD_D_INNER // SSD_HEAD_DIM
SSD_GROUPS = 4
SSD_STATE = 128
SSD_CONV = 5
SSD_CONV_PAD = SSD_CONV // 2
SSD_CHUNK = 128
SSD_GN = SSD_GROUPS * SSD_STATE
SSD_CONV_CH = SSD_D_INNER + 2 * SSD_GN
SSD_IN_COLS = 2 * SSD_D_INNER + SSD_CONV_CH + 2 * SSD_HEADS
DT_MIN = 0.001
DT_MAX = 0.1

ATTN_HEAD_DIM = 64
ATTN_HEADS = D_MODEL // ATTN_HEAD_DIM
ATTN_KV_HEADS = 4
ATTN_GROUP = ATTN_HEADS // ATTN_KV_HEADS
ATTN_Q_COLS = ATTN_HEADS * ATTN_HEAD_DIM
ATTN_KV_COLS = ATTN_KV_HEADS * ATTN_HEAD_DIM
ATTN_QKV_COLS = ATTN_Q_COLS + 2 * ATTN_KV_COLS
Q_BLOCK = 128
ROPE_THETA = 10000.0
ROPE_PAIRS_PER_AXIS = ATTN_HEAD_DIM // 4

N_EXPERTS = 32
TOP_K = 4
D_EXPERT = D_MODEL
SWIGLU_ALPHA = 1.702
SWIGLU_LIMIT = 7.0
MOE_BLOCK = 256

N_MOD = 6
MOD_INIT_SCALE = 0.5
NORM_EPS = 1e-6

kernel_name = 'hybrid_ssd_gqa_moe_diffusion_trunk'


def rms_norm(x, g):
    xf = x.astype(jnp.float32)
    y = xf * lax.rsqrt(jnp.mean(xf * xf, axis=-1, keepdims=True) + NORM_EPS)
    return (y * g.astype(jnp.float32)).astype(x.dtype)


def depthwise_conv(u, w, b):
    out = lax.conv_general_dilated(u, w[:, None, :].astype(u.dtype), window_strides=(1,),
                                   padding=[(SSD_CONV_PAD, SSD_CONV_PAD)],
                                   dimension_numbers=('NWC', 'WIO', 'NWC'),
                                   feature_group_count=u.shape[-1])
    return out + b


def ssd_scan(xs, dt, a_head, bm, cm, h0):
    f32 = jnp.float32
    b, L, H, P = xs.shape
    G, N = bm.shape[2], bm.shape[3]
    J = H // G
    nc = L // SSD_CHUNK
    dtf = dt.astype(f32)
    la = (dtf * a_head.astype(f32)).reshape(b, nc, SSD_CHUNK, G, J)
    xd = (xs.astype(f32) * dtf[..., None]).reshape(b, nc, SSD_CHUNK, G, J, P)
    bc = bm.astype(f32).reshape(b, nc, SSD_CHUNK, G, N)
    cc = cm.astype(f32).reshape(b, nc, SSD_CHUNK, G, N)
    cs = jnp.cumsum(la, axis=2)
    lower = jnp.tril(jnp.ones((SSD_CHUNK, SSD_CHUNK), bool))
    seg = cs[:, :, :, None] - cs[:, :, None, :]
    decay = jnp.exp(jnp.where(lower[None, None, :, :, None, None], seg, -jnp.inf))
    cb = jnp.einsum('bclgn,bcsgn->bclsg', cc, bc)
    y_diag = jnp.einsum('bclsg,bclsgj,bcsgjp->bclgjp', cb, decay, xd)
    to_end = jnp.exp(cs[:, :, -1:] - cs)
    states = jnp.einsum('bclgn,bclgj,bclgjp->bcgjpn', bc, to_end, xd)
    chunk_decay = jnp.exp(cs[:, :, -1])

    def step(h, inp):
        s, dcy = inp
        return h * dcy[..., None, None] + s, h

    h_last, h_in = lax.scan(step, h0.astype(f32),
                            (jnp.moveaxis(states, 1, 0), jnp.moveaxis(chunk_decay, 1, 0)))
    h_in = jnp.moveaxis(h_in, 0, 1)
    y_off = jnp.einsum('bclgn,bcgjpn,bclgj->bclgjp', cc, h_in, jnp.exp(cs))
    y = (y_diag + y_off).reshape(b, L, H, P).astype(xs.dtype)
    return y, h_last


def ssd_project(h, in_w, conv_w, conv_b, dt_bias):
    b, L, _ = h.shape
    zxbcdt = h @ in_w
    z = zxbcdt[..., :2 * SSD_D_INNER].reshape(b, L, 2, SSD_D_INNER)
    xbc = jax.nn.silu(depthwise_conv(zxbcdt[..., 2 * SSD_D_INNER:2 * SSD_D_INNER + SSD_CONV_CH], conv_w, conv_b))
    dt = jax.nn.softplus(zxbcdt[..., 2 * SSD_D_INNER + SSD_CONV_CH:].reshape(b, L, 2, SSD_HEADS) + dt_bias)
    xs = xbc[..., :SSD_D_INNER].reshape(b, L, SSD_HEADS, SSD_HEAD_DIM)
    bm = xbc[..., SSD_D_INNER:SSD_D_INNER + SSD_GN].reshape(b, L, SSD_GROUPS, SSD_STATE)
    cm = xbc[..., SSD_D_INNER + SSD_GN:].reshape(b, L, SSD_GROUPS, SSD_STATE)
    return z, xs, dt, bm, cm


def ssd_bidirectional(proj, a_log, d_skip, h0_fwd, h0_bwd):
    z, xs, dt, bm, cm = proj
    b, L = xs.shape[0], xs.shape[1]
    a = -jnp.exp(a_log.astype(jnp.float32))
    y_fwd, h_fwd = ssd_scan(xs, dt[:, :, 0], a[0], bm, cm, h0_fwd)
    rev = lambda t: jnp.flip(t, axis=1)
    y_bwd, h_bwd = ssd_scan(rev(xs), rev(dt[:, :, 1]), a[1], rev(bm), rev(cm), h0_bwd)
    y_bwd = rev(y_bwd)
    y_fwd = (y_fwd + d_skip[0][:, None] * xs).reshape(b, L, SSD_D_INNER)
    y_bwd = (y_bwd + d_skip[1][:, None] * xs).reshape(b, L, SSD_D_INNER)
    y = y_fwd * jax.nn.silu(z[:, :, 0]) + y_bwd * jax.nn.silu(z[:, :, 1])
    return y, h_fwd, h_bwd


def ssd_mixer(h_lat, h_ctx, in_w, conv_w, conv_b, dt_bias, a_log, d_skip, norm_g, out_w, want_ctx):
    b = h_ctx.shape[0]
    zero = jnp.zeros((b, SSD_GROUPS, SSD_HEADS // SSD_GROUPS, SSD_HEAD_DIM, SSD_STATE), jnp.float32)
    y_c, hc_fwd, hc_bwd = ssd_bidirectional(ssd_project(h_ctx, in_w, conv_w, conv_b, dt_bias),
                                            a_log, d_skip, zero, zero)
    y_l, _, _ = ssd_bidirectional(ssd_project(h_lat, in_w, conv_w, conv_b, dt_bias),
                                  a_log, d_skip, hc_fwd, hc_bwd)

    def finish(y):
        bb, L, _ = y.shape
        yn = rms_norm(y.reshape(bb, L, SSD_GROUPS, SSD_D_INNER // SSD_GROUPS),
                      norm_g.reshape(SSD_GROUPS, SSD_D_INNER // SSD_GROUPS))
        return yn.reshape(bb, L, SSD_D_INNER) @ out_w

    out_c = finish(y_c) if want_ctx else None
    return finish(y_l), out_c


def apply_rope(x, cos, sin):
    half = x.shape[-1] // 2
    shape = (1, x.shape[1]) + (1,) * (x.ndim - 3) + (half,)
    cos = cos.reshape(shape)
    sin = sin.reshape(shape)
    xf = x.astype(jnp.float32)
    x1, x2 = xf[..., :half], xf[..., half:]
    return jnp.concatenate([x1 * cos - x2 * sin, x2 * cos + x1 * sin], axis=-1).astype(x.dtype)


def sdpa(q, k, v):
    s = jnp.einsum('bqhgd,bkhd->bhgqk', q, k).astype(jnp.float32) * (ATTN_HEAD_DIM ** -0.5)
    p = jax.nn.softmax(s, axis=-1).astype(v.dtype)
    return jnp.einsum('bhgqk,bkhd->bqhgd', p, v)


def attention_mixer(h_lat, h_ctx, rope_cos, rope_sin, qkv_w, q_norm_g, k_norm_g, out_w, want_ctx):
    def project(h):
        b, L, _ = h.shape
        t = h @ qkv_w
        q = t[..., :ATTN_Q_COLS].reshape(b, L, ATTN_KV_HEADS, ATTN_GROUP, ATTN_HEAD_DIM)
        k = t[..., ATTN_Q_COLS:ATTN_Q_COLS + ATTN_KV_COLS].reshape(b, L, ATTN_KV_HEADS, ATTN_HEAD_DIM)
        v = t[..., ATTN_Q_COLS + ATTN_KV_COLS:].reshape(b, L, ATTN_KV_HEADS, ATTN_HEAD_DIM)
        return rms_norm(q, q_norm_g), rms_norm(k, k_norm_g), v

    q_l, k_l, v_l = project(h_lat)
    q_c, k_c, v_c = project(h_ctx)
    q_l = apply_rope(q_l, rope_cos, rope_sin)
    k_l = apply_rope(k_l, rope_cos, rope_sin)
    k_all = jnp.concatenate([k_c, k_l], axis=1)
    v_all = jnp.concatenate([v_c, v_l], axis=1)
    b, L = h_lat.shape[0], h_lat.shape[1]
    nb = L // Q_BLOCK
    q_blocks = jnp.moveaxis(q_l.reshape(b, nb, Q_BLOCK, ATTN_KV_HEADS, ATTN_GROUP, ATTN_HEAD_DIM), 1, 0)
    o = lax.map(lambda qb: sdpa(qb, k_all, v_all), q_blocks)
    o = jnp.moveaxis(o, 0, 1).reshape(b, L, ATTN_Q_COLS)
    out_l = o @ out_w
    out_c = sdpa(q_c, k_c, v_c).reshape(b, h_ctx.shape[1], ATTN_Q_COLS) @ out_w if want_ctx else None
    return out_l, out_c


def moe_ffn(h, router_w, router_b, w_gu, b_gu, w_dn, b_dn):
    t, d = h.shape
    logits = (h @ router_w + router_b).astype(jnp.float32)
    top_val, top_idx = lax.top_k(logits, TOP_K)
    gate = jax.nn.softmax(top_val, axis=-1).astype(h.dtype)
    n_assign = t * TOP_K
    flat_e = top_idx.reshape(-1)
    flat_tok = jnp.arange(n_assign, dtype=jnp.int32) // TOP_K
    order = jnp.argsort(flat_e)
    sorted_e = flat_e[order]
    counts = jnp.bincount(flat_e, length=N_EXPERTS)
    padded = (counts + MOE_BLOCK - 1) // MOE_BLOCK * MOE_BLOCK
    start = jnp.cumsum(counts) - counts
    pad_end = jnp.cumsum(padded)
    pad_start = pad_end - padded
    dest = pad_start[sorted_e] + jnp.arange(n_assign, dtype=jnp.int32) - start[sorted_e]
    n_blocks = -(-n_assign // MOE_BLOCK) + N_EXPERTS
    n_rows = n_blocks * MOE_BLOCK
    row_tok = jnp.full((n_rows,), t, jnp.int32).at[dest].set(flat_tok[order])
    row_gate = jnp.zeros((n_rows,), h.dtype).at[dest].set(gate.reshape(-1)[order])
    block_exp = jnp.minimum(jnp.searchsorted(pad_end, jnp.arange(n_blocks, dtype=jnp.int32) * MOE_BLOCK,
                                             side='right'), N_EXPERTS - 1)
    h_pad = jnp.concatenate([h, jnp.zeros((1, d), h.dtype)], axis=0)
    xin = h_pad[row_tok].reshape(n_blocks, MOE_BLOCK, d)

    def expert_block(args):
        xb, e = args
        gu = xb @ w_gu[e] + b_gu[e]
        g, u = gu[:, :D_EXPERT], gu[:, D_EXPERT:]
        g = jnp.minimum(g, SWIGLU_LIMIT)
        u = jnp.clip(u, -SWIGLU_LIMIT, SWIGLU_LIMIT)
        return (g * jax.nn.sigmoid(SWIGLU_ALPHA * g) * (u + 1.0)) @ w_dn[e] + b_dn[e]

    y_rows = lax.map(expert_block, (xin, block_exp)).reshape(n_rows, d) * row_gate[:, None]
    return jnp.zeros_like(h_pad).at[row_tok].add(y_rows)[:t]


def setup_inputs(seed: int = 0) -> dict:
    key = jax.random.key(seed)
    ks = jax.random.split(key, 28)
    f32 = jnp.float32
    D = D_MODEL

    def dense(k, shape, fan_in):
        return jax.random.normal(k, shape, f32) * (fan_in ** -0.5)

    def gain(k, shape):
        return 1.0 + 0.05 * jax.random.normal(k, shape, f32)

    def small(k, shape, s):
        return s * jax.random.normal(k, shape, f32)

    dt0 = jnp.exp(jax.random.uniform(ks[12], (N_SSD_LAYERS, 2, SSD_HEADS), f32,
                                     minval=math.log(DT_MIN), maxval=math.log(DT_MAX)))
    return {
        'x': jax.random.normal(ks[0], (BATCH, SEQ, D), f32),
        'c': jax.random.normal(ks[1], (BATCH, D), f32),
        'ctx': jax.random.normal(ks[2], (BATCH, CTX_LEN, D), f32),
        'c_ctx': jax.random.normal(ks[3], (D,), f32),
        'mod_w': dense(ks[4], (DEPTH, D, N_MOD * D), D) * MOD_INIT_SCALE,
        'mod_b': small(ks[5], (DEPTH, N_MOD * D), 0.02),
        'norm1_g': gain(ks[6], (DEPTH, D)),
        'norm2_g': gain(ks[7], (DEPTH, D)),
        'final_norm_g': gain(ks[8], (D,)),
        'ssd_in_w': dense(ks[9], (N_SSD_LAYERS, D, SSD_IN_COLS), D),
        'ssd_conv_w': dense(ks[10], (N_SSD_LAYERS, SSD_CONV, SSD_CONV_CH), SSD_CONV),
        'ssd_conv_b': small(ks[11], (N_SSD_LAYERS, SSD_CONV_CH), 0.02),
        'ssd_dt_bias': dt0 + jnp.log(-jnp.expm1(-dt0)),
        'ssd_a_log': jnp.log(jax.random.uniform(ks[13], (N_SSD_LAYERS, 2, SSD_HEADS), f32, minval=1.0, maxval=16.0)),
        'ssd_d': gain(ks[14], (N_SSD_LAYERS, 2, SSD_HEADS)),
        'ssd_norm_g': gain(ks[15], (N_SSD_LAYERS, SSD_D_INNER)),
        'ssd_out_w': dense(ks[16], (N_SSD_LAYERS, SSD_D_INNER, D), SSD_D_INNER),
        'attn_qkv_w': dense(ks[17], (N_ATTN_LAYERS, D, ATTN_QKV_COLS), D),
        'attn_q_norm_g': gain(ks[18], (N_ATTN_LAYERS, ATTN_HEAD_DIM)),
        'attn_k_norm_g': gain(ks[19], (N_ATTN_LAYERS, ATTN_HEAD_DIM)),
        'attn_out_w': dense(ks[20], (N_ATTN_LAYERS, ATTN_Q_COLS, D), ATTN_Q_COLS),
        'moe_router_w': dense(ks[21], (DEPTH, D, N_EXPERTS), D),
        'moe_router_b': small(ks[22], (DEPTH, N_EXPERTS), 0.01),
        'moe_w_gate_up': dense(ks[23], (DEPTH, N_EXPERTS, D, 2 * D_EXPERT), D),
        'moe_b_gate_up': small(ks[24], (DEPTH, N_EXPERTS, 2 * D_EXPERT), 0.01),
        'moe_w_down': dense(ks[25], (DEPTH, N_EXPERTS, D_EXPERT, D), D_EXPERT),
        'moe_b_down': small(ks[26], (DEPTH, N_EXPERTS, D), 0.01),
    }


def reference(x, c, ctx, c_ctx, mod_w, mod_b, norm1_g, norm2_g, final_norm_g,
              ssd_in_w, ssd_conv_w, ssd_conv_b, ssd_dt_bias, ssd_a_log, ssd_d, ssd_norm_g, ssd_out_w,
              attn_qkv_w, attn_q_norm_g, attn_k_norm_g, attn_out_w,
              moe_router_w, moe_router_b, moe_w_gate_up, moe_b_gate_up, moe_w_down, moe_b_down):
    lat, cx = x, ctx
    rows = lat.shape[1] // GRID_W
    n_lat = rows * GRID_W
    pos_row = jnp.repeat(jnp.arange(rows, dtype=jnp.float32), GRID_W)
    pos_col = (jnp.arange(n_lat, dtype=jnp.int32) % GRID_W).astype(jnp.float32)
    inv_freq = jnp.power(ROPE_THETA, -jnp.arange(ROPE_PAIRS_PER_AXIS, dtype=jnp.float32) / ROPE_PAIRS_PER_AXIS)
    ang = jnp.concatenate([pos_row[:, None] * inv_freq, pos_col[:, None] * inv_freq], axis=-1)
    rope_cos, rope_sin = jnp.cos(ang), jnp.sin(ang)

    for i in range(DEPTH):
        last = i == DEPTH - 1
        j = i // N_MIXERS
        m_l = (jax.nn.silu(c) @ mod_w[i] + mod_b[i]).reshape(-1, N_MOD, 1, D_MODEL)
        m_c = (jax.nn.silu(c_ctx) @ mod_w[i] + mod_b[i]).reshape(1, N_MOD, 1, D_MODEL)
        h_l = rms_norm(lat, norm1_g[i]) * (1.0 + m_l[:, 1]) + m_l[:, 0]
        h_c = rms_norm(cx, norm1_g[i]) * (1.0 + m_c[:, 1]) + m_c[:, 0]
        if i % N_MIXERS == 0:
            y_l, y_c = ssd_mixer(h_l, h_c, ssd_in_w[j], ssd_conv_w[j], ssd_conv_b[j], ssd_dt_bias[j],
                                 ssd_a_log[j], ssd_d[j], ssd_norm_g[j], ssd_out_w[j], not last)
        else:
            y_l, y_c = attention_mixer(h_l, h_c, rope_cos, rope_sin, attn_qkv_w[j], attn_q_norm_g[j],
                                       attn_k_norm_g[j], attn_out_w[j], not last)
        lat = lat + m_l[:, 2] * y_l
        f_l = rms_norm(lat, norm2_g[i]) * (1.0 + m_l[:, 4]) + m_l[:, 3]
        if last:
            out_l = moe_ffn(f_l.reshape(-1, D_MODEL), moe_router_w[i], moe_router_b[i], moe_w_gate_up[i],
                            moe_b_gate_up[i], moe_w_down[i], moe_b_down[i])
            lat = lat + m_l[:, 5] * out_l.reshape(lat.shape)
        else:
            cx = cx + m_c[:, 2] * y_c
            f_c = rms_norm(cx, norm2_g[i]) * (1.0 + m_c[:, 4]) + m_c[:, 3]
            n_c = f_c.shape[0] * f_c.shape[1]
            out = moe_ffn(jnp.concatenate([f_c.reshape(-1, D_MODEL), f_l.reshape(-1, D_MODEL)], axis=0),
                          moe_router_w[i], moe_router_b[i], moe_w_gate_up[i], moe_b_gate_up[i],
                          moe_w_down[i], moe_b_down[i])
            cx = cx + m_c[:, 5] * out[:n_c].reshape(cx.shape)
            lat = lat + m_l[:, 5] * out[n_c:].reshape(lat.shape)
    return rms_norm(lat, final_norm_g)
```

```python
import functools
import math

import jax
import jax.numpy as jnp
from jax import lax
from jax.experimental import pallas as pl
from jax.experimental.pallas import tpu as pltpu

F32 = jnp.float32
BF16 = jnp.bfloat16
I32 = jnp.int32

GRID_W = 64
SSD_HEAD_DIM = 64
SSD_GROUPS = 4
SSD_STATE = 128
SSD_CONV = 5
SSD_CHUNK = 128
ATTN_HEAD_DIM = 64
ATTN_KV_HEADS = 4
ROPE_THETA = 10000.0
TOP_K = 4
SWIGLU_ALPHA = 1.702
SWIGLU_LIMIT = 7.0
N_MOD = 6
NORM_EPS = 1e-6

LANES = 128
SUBLANES = 8
VMEM_LIMIT = 48 * 1024 * 1024

ROW_TILE = 256
MOE_BLOCK = 256
ATTN_TQ = 128
ATTN_BK = 768

NT_DIMS = (((1,), (1,)), ((), ()))


def _params(n_grid, vmem=VMEM_LIMIT):
    return pltpu.CompilerParams(dimension_semantics=("arbitrary",) * n_grid, vmem_limit_bytes=vmem)


def _split3(x):
    h = x.astype(BF16)
    r = x - h.astype(F32)
    m = r.astype(BF16)
    l = (r - m.astype(F32)).astype(BF16)
    return h, m, l


def _dot(a, b):
    return jnp.dot(a, b, preferred_element_type=F32)


def _dot_f32_by_exact(x, w):
    h, m, l = _split3(x)
    return _dot(h, w) + _dot(m, w) + _dot(l, w)


def _exact_by_dot_f32(w, x):
    h, m, l = _split3(x)
    return _dot(w, h) + _dot(w, m) + _dot(w, l)


def _sigmoid(x):
    return 1.0 / (1.0 + jnp.exp(-x))


def _rms(x, g):
    ms = jnp.mean(x * x, axis=-1, keepdims=True)
    return x * lax.rsqrt(ms + NORM_EPS) * g


def _mod_kernel(c_ref, w_ref, b_ref, o_ref):
    c = c_ref[...]
    s = c * _sigmoid(c)
    o_ref[0] = jnp.dot(s, w_ref[0], preferred_element_type=F32, precision=lax.Precision.HIGHEST) + b_ref[0]


def _modulation(c_pad, mod_w, mod_b):
    depth, d, n = mod_w.shape
    rows = c_pad.shape[0]
    tn = 1024
    return pl.pallas_call(
        _mod_kernel,
        grid=(depth, n // tn),
        in_specs=[pl.BlockSpec((rows, d), lambda i, j: (0, 0)),
                  pl.BlockSpec((1, d, tn), lambda i, j: (i, 0, j)),
                  pl.BlockSpec((1, 1, tn), lambda i, j: (i, 0, j))],
        out_specs=pl.BlockSpec((1, rows, tn), lambda i, j: (i, 0, j)),
        out_shape=jax.ShapeDtypeStruct((depth, rows, n), F32),
        compiler_params=_params(2),
        name="modulation",
    )(c_pad, mod_w, mod_b.reshape(depth, 1, n))


def _mod_spec(n_ctx_tiles, ctx_row):
    def index(b, j):
        return (jnp.where(j < n_ctx_tiles, ctx_row, b), 0, 0)
    return index


def _norm_mod_kernel(x_ref, g_ref, m_ref, o_ref, *, shift, scale):
    y = _rms(x_ref[0], g_ref[...])
    o_ref[0] = (y * (1.0 + m_ref[0, scale:scale + 1, :]) + m_ref[0, shift:shift + 1, :]).astype(o_ref.dtype)


def _norm_mod(xall, g, mod, n_ctx_tiles, ctx_row, shift, scale):
    b, s, d = xall.shape
    return pl.pallas_call(
        functools.partial(_norm_mod_kernel, shift=shift, scale=scale),
        grid=(b, s // ROW_TILE),
        in_specs=[pl.BlockSpec((1, ROW_TILE, d), lambda i, j: (i, j, 0)),
                  pl.BlockSpec((1, d), lambda i, j: (0, 0)),
                  pl.BlockSpec((1, N_MOD, d), _mod_spec(n_ctx_tiles, ctx_row))],
        out_specs=pl.BlockSpec((1, ROW_TILE, d), lambda i, j: (i, j, 0)),
        out_shape=jax.ShapeDtypeStruct((b, s, d), BF16),
        compiler_params=_params(2),
        name="norm_mod",
    )(xall, g.reshape(1, d), mod)


def _mm_kernel(x_ref, w_ref, o_ref):
    o_ref[...] = _dot(x_ref[...], w_ref[...]).astype(o_ref.dtype)


def _matmul(x, w, out_dtype, name, tm=512, tn=1024):
    m, k = x.shape
    n = w.shape[1]
    tn = min(tn, n)
    return pl.pallas_call(
        _mm_kernel,
        grid=(n // tn, m // tm),
        in_specs=[pl.BlockSpec((tm, k), lambda c, r: (r, 0)),
                  pl.BlockSpec((k, tn), lambda c, r: (0, c))],
        out_specs=pl.BlockSpec((tm, tn), lambda c, r: (r, c)),
        out_shape=jax.ShapeDtypeStruct((m, n), out_dtype),
        compiler_params=_params(2),
        name=name,
    )(x, w)


def _conv_kernel(prev_ref, cur_ref, next_ref, w_ref, b_ref, o_ref, ext_ref, *, n_ctx_tiles, n_tiles):
    j = pl.program_id(1)
    pad = SSD_CONV // 2
    first = jnp.logical_or(j == 0, j == n_ctx_tiles)
    last = jnp.logical_or(j == n_ctx_tiles - 1, j == n_tiles - 1)
    ext_ref[0:SUBLANES, :] = jnp.where(first, 0.0, prev_ref[0])
    ext_ref[SUBLANES:SUBLANES + ROW_TILE, :] = cur_ref[0]
    ext_ref[SUBLANES + ROW_TILE:2 * SUBLANES + ROW_TILE, :] = jnp.where(last, 0.0, next_ref[0])
    acc = b_ref[...] + w_ref[0:1, :] * ext_ref[SUBLANES - pad:SUBLANES - pad + ROW_TILE, :]
    for k in range(1, SSD_CONV):
        acc = acc + w_ref[k:k + 1, :] * ext_ref[SUBLANES - pad + k:SUBLANES - pad + k + ROW_TILE, :]
    o_ref[0] = (acc * _sigmoid(acc)).astype(o_ref.dtype)


def _conv_silu(xbc, conv_w, conv_b, n_ctx_tiles):
    b, s, c = xbc.shape
    tc = 1024
    n_tiles = s // ROW_TILE
    rb = ROW_TILE // SUBLANES
    n_rb = s // SUBLANES
    return pl.pallas_call(
        functools.partial(_conv_kernel, n_ctx_tiles=n_ctx_tiles, n_tiles=n_tiles),
        grid=(b, n_tiles, c // tc),
        in_specs=[pl.BlockSpec((1, SUBLANES, tc), lambda i, j, k: (i, jnp.maximum(j * rb - 1, 0), k)),
                  pl.BlockSpec((1, ROW_TILE, tc), lambda i, j, k: (i, j, k)),
                  pl.BlockSpec((1, SUBLANES, tc), lambda i, j, k: (i, jnp.minimum((j + 1) * rb, n_rb - 1), k)),
                  pl.BlockSpec((SSD_CONV, tc), lambda i, j, k: (0, k)),
                  pl.BlockSpec((1, tc), lambda i, j, k: (0, k))],
        out_specs=pl.BlockSpec((1, ROW_TILE, tc), lambda i, j, k: (i, j, k)),
        out_shape=jax.ShapeDtypeStruct((b, s, c), BF16),
        scratch_shapes=[pltpu.VMEM((ROW_TILE + 2 * SUBLANES, tc), F32)],
        compiler_params=_params(3),
        name="conv_silu",
    )(xbc, xbc, xbc, conv_w, conv_b.reshape(1, c))


def _ssd_kernel(xs_ref, b_ref, c_ref, raw_ref, sel_ref, bias_ref, a_ref, tri_ref, y_ref, ht_ref, *,
                reverse, heads_per_group):
    L = SSD_CHUNK
    P = SSD_HEAD_DIM

    @pl.when(pl.program_id(2) == 0)
    def _():
        ht_ref[...] = jnp.zeros_like(ht_ref)

    raw = _dot_f32_by_exact(raw_ref[0], sel_ref[0])
    z = raw + bias_ref[0]
    dt = jnp.maximum(z, 0.0) + jnp.log(1.0 + jnp.exp(-jnp.abs(z)))
    la = dt * a_ref[0]
    cs = _exact_by_dot_f32(tri_ref[...], la)
    cs_t = cs.T
    dt_t = dt.T
    bg = b_ref[0]
    cg = c_ref[0]
    cb = lax.dot_general(cg, bg, NT_DIMS, preferred_element_type=F32)
    bg_t = bg.astype(F32).T
    cf = cg.astype(F32)
    li = lax.broadcasted_iota(I32, (L, L), 0)
    si = lax.broadcasted_iota(I32, (L, L), 1)
    valid = (li <= si) if reverse else (li >= si)
    end = 0 if reverse else L - 1
    ys = []
    for j in range(heads_per_group):
        colb = jnp.broadcast_to(cs[:, j:j + 1], (L, L))
        rowb = jnp.broadcast_to(cs_t[j:j + 1, :], (L, L))
        decay = jnp.exp(jnp.where(valid, colb - rowb, -jnp.inf))
        m = (cb * decay * dt_t[j:j + 1, :]).astype(BF16)
        ce = (cf * jnp.exp(colb)).astype(BF16)
        xj = xs_ref[0, :, j * P:(j + 1) * P]
        ht = ht_ref[j]
        lhs = jnp.concatenate([m, ce], axis=1)
        rhs = jnp.concatenate([xj, ht.astype(BF16)], axis=0)
        ys.append(_dot(lhs, rhs))
        cse = cs_t[j:j + 1, end:end + 1]
        w_row = jnp.exp(cse - cs_t[j:j + 1, :]) * dt_t[j:j + 1, :]
        bw = (bg_t * w_row).astype(BF16)
        ht_ref[j] = ht * jnp.exp(cse) + _dot(bw, xj)
    y_ref[0] = jnp.concatenate(ys, axis=1).astype(y_ref.dtype)


def _ssd_scan(xc, dtraw, sel, bias_g, a_g, tri, n_ctx_chunks, reverse, d_inner):
    b, s, _ = xc.shape
    n_chunks = s // SSD_CHUNK
    heads = d_inner // SSD_HEAD_DIM
    hpg = heads // SSD_GROUPS
    gw = hpg * SSD_HEAD_DIM
    x_blocks = d_inner // SSD_STATE
    d = 1 if reverse else 0

    def chunk(i):
        if not reverse:
            return i
        return jnp.where(i < n_ctx_chunks, n_ctx_chunks - 1 - i, n_chunks - 1 + n_ctx_chunks - i)

    return pl.pallas_call(
        functools.partial(_ssd_kernel, reverse=reverse, heads_per_group=hpg),
        grid=(b, SSD_GROUPS, n_chunks),
        in_specs=[pl.BlockSpec((1, SSD_CHUNK, gw), lambda i, g, c: (i, chunk(c), g)),
                  pl.BlockSpec((1, SSD_CHUNK, SSD_STATE), lambda i, g, c: (i, chunk(c), x_blocks + g)),
                  pl.BlockSpec((1, SSD_CHUNK, SSD_STATE), lambda i, g, c: (i, chunk(c), x_blocks + SSD_GROUPS + g)),
                  pl.BlockSpec((1, SSD_CHUNK, LANES), lambda i, g, c: (i, chunk(c), 0)),
                  pl.BlockSpec((1, LANES, LANES), lambda i, g, c: (d * SSD_GROUPS + g, 0, 0)),
                  pl.BlockSpec((1, 1, LANES), lambda i, g, c: (d * SSD_GROUPS + g, 0, 0)),
                  pl.BlockSpec((1, 1, LANES), lambda i, g, c: (d * SSD_GROUPS + g, 0, 0)),
                  pl.BlockSpec((SSD_CHUNK, SSD_CHUNK), lambda i, g, c: (0, 0))],
        out_specs=pl.BlockSpec((1, SSD_CHUNK, gw), lambda i, g, c: (i, chunk(c), g)),
        out_shape=jax.ShapeDtypeStruct((b, s, d_inner), BF16),
        scratch_shapes=[pltpu.VMEM((hpg, SSD_STATE, SSD_HEAD_DIM), F32)],
        compiler_params=_params(3),
        name="ssd_scan_bwd" if reverse else "ssd_scan_fwd",
    )(xc, xc, xc, dtraw, sel, bias_g, a_g, tri)


def _ssd_out_kernel(yf_ref, yb_ref, zf_ref, zb_ref, xs_ref, dsk_ref, ng_ref, w_ref, x_ref, m_ref, o_ref, *, groups):
    xs = xs_ref[0].astype(F32)
    zf = zf_ref[0].astype(F32)
    zb = zb_ref[0].astype(F32)
    y = ((yf_ref[0].astype(F32) + dsk_ref[0:1, :] * xs) * (zf * _sigmoid(zf))
         + (yb_ref[0].astype(F32) + dsk_ref[1:2, :] * xs) * (zb * _sigmoid(zb)))
    gw = y.shape[1] // groups
    parts = []
    for g in range(groups):
        parts.append(_rms(y[:, g * gw:(g + 1) * gw], ng_ref[:, g * gw:(g + 1) * gw]).astype(BF16))
    out = _dot(jnp.concatenate(parts, axis=1), w_ref[...])
    o_ref[0] = x_ref[0] + m_ref[0, 2:3, :] * out


def _ssd_out(yf, yb, z, xc, dskip, norm_g, out_w, xall, mod, n_ctx_tiles, ctx_row):
    b, s, d = xall.shape
    di = yf.shape[2]
    row = lambda i, j: (i, j, 0)
    return pl.pallas_call(
        functools.partial(_ssd_out_kernel, groups=SSD_GROUPS),
        grid=(b, s // ROW_TILE),
        in_specs=[pl.BlockSpec((1, ROW_TILE, di), row),
                  pl.BlockSpec((1, ROW_TILE, di), row),
                  pl.BlockSpec((1, ROW_TILE, di), row),
                  pl.BlockSpec((1, ROW_TILE, di), lambda i, j: (i, j, 1)),
                  pl.BlockSpec((1, ROW_TILE, di), row),
                  pl.BlockSpec((2, di), lambda i, j: (0, 0)),
                  pl.BlockSpec((1, di), lambda i, j: (0, 0)),
                  pl.BlockSpec((di, d), lambda i, j: (0, 0)),
                  pl.BlockSpec((1, ROW_TILE, d), row),
                  pl.BlockSpec((1, N_MOD, d), _mod_spec(n_ctx_tiles, ctx_row))],
        out_specs=pl.BlockSpec((1, ROW_TILE, d), row),
        out_shape=jax.ShapeDtypeStruct((b, s, d), F32),
        compiler_params=_params(2),
        name="ssd_out",
    )(yf, yb, z, z, xc, dskip, norm_g.reshape(1, di), out_w, xall, mod)


def _qkv_kernel(h_ref, w_ref, ones_ref, qg_ref, kg_ref, cos_ref, sin_ref, q_ref, k_ref, v_ref, *, n_q, n_kv):
    hd = ATTN_HEAD_DIM
    t = _dot(h_ref[0], w_ref[...])
    blk = 2 * LANES
    cos = jnp.concatenate([cos_ref[0], cos_ref[0]], axis=1)
    sin = jnp.concatenate([sin_ref[0], sin_ref[0]], axis=1)
    lane = lax.broadcasted_iota(I32, (t.shape[0], blk), 1)
    first_half = (lane % hd) < (hd // 2)
    outs = []
    for c in range((n_q + n_kv) // blk):
        x = t[:, c * blk:(c + 1) * blk]
        sq = x * x
        hi = sq.astype(BF16)
        lo = (sq - hi.astype(F32)).astype(BF16)
        ss = _dot(hi, ones_ref[...]) + _dot(lo, ones_ref[...])
        g = qg_ref[...] if c * blk < n_q else kg_ref[...]
        xn = x * lax.rsqrt(ss * (1.0 / hd) + NORM_EPS) * g
        partner = jnp.where(first_half, pltpu.roll(xn, blk - hd // 2, axis=1), pltpu.roll(xn, hd // 2, axis=1))
        outs.append(xn * cos + partner * sin)
    q = jnp.concatenate(outs[:n_q // blk], axis=1) * (hd ** -0.5)
    q_ref[0] = q.astype(q_ref.dtype)
    kk = jnp.concatenate(outs[n_q // blk:], axis=1)
    vv = t[:, n_q + n_kv:]
    for h in range(n_kv // hd):
        k_ref[0, h] = kk[:, h * hd:(h + 1) * hd].astype(k_ref.dtype)
        v_ref[0, h] = vv[:, h * hd:(h + 1) * hd].astype(v_ref.dtype)


def _qkv(h, qkv_w, ones_blk, qg, kg, cos_t, sin_t, n_q, n_kv):
    b, s, d = h.shape
    kvh = n_kv // ATTN_HEAD_DIM
    blk = 2 * LANES
    return pl.pallas_call(
        functools.partial(_qkv_kernel, n_q=n_q, n_kv=n_kv),
        grid=(b, s // ROW_TILE),
        in_specs=[pl.BlockSpec((1, ROW_TILE, d), lambda i, j: (i, j, 0)),
                  pl.BlockSpec((d, n_q + 2 * n_kv), lambda i, j: (0, 0)),
                  pl.BlockSpec((blk, blk), lambda i, j: (0, 0)),
                  pl.BlockSpec((1, blk), lambda i, j: (0, 0)),
                  pl.BlockSpec((1, blk), lambda i, j: (0, 0)),
                  pl.BlockSpec((1, ROW_TILE, LANES), lambda i, j: (0, j, 0)),
                  pl.BlockSpec((1, ROW_TILE, LANES), lambda i, j: (0, j, 0))],
        out_specs=[pl.BlockSpec((1, ROW_TILE, n_q), lambda i, j: (i, j, 0)),
                   pl.BlockSpec((1, kvh, ROW_TILE, ATTN_HEAD_DIM), lambda i, j: (i, 0, j, 0)),
                   pl.BlockSpec((1, kvh, ROW_TILE, ATTN_HEAD_DIM), lambda i, j: (i, 0, j, 0))],
        out_shape=[jax.ShapeDtypeStruct((b, s, n_q), BF16),
                   jax.ShapeDtypeStruct((b, kvh, s, ATTN_HEAD_DIM), BF16),
                   jax.ShapeDtypeStruct((b, kvh, s, ATTN_HEAD_DIM), BF16)],
        compiler_params=_params(2),
        name="qkv_rope",
    )(h, qkv_w, ones_blk, qg, kg, cos_t, sin_t)


def _flash_kernel(q_ref, k_ref, v_ref, o_ref, *, group, n_kv_blocks):
    hd = ATTN_HEAD_DIM
    tq = q_ref.shape[1]
    q = q_ref[0]
    qs = jnp.concatenate([q[:, g * hd:(g + 1) * hd] for g in range(group)], axis=0)
    rows = group * tq

    def body(i, carry):
        m, l, acc = carry
        start = pl.multiple_of(i * ATTN_BK, ATTN_BK)
        kb = k_ref[0, 0, pl.ds(start, ATTN_BK), :]
        vb = v_ref[0, 0, pl.ds(start, ATTN_BK), :]
        s = lax.dot_general(qs, kb, NT_DIMS, preferred_element_type=F32)
        m_new = jnp.maximum(m, jnp.max(s, axis=-1, keepdims=True))
        alpha = jnp.exp(m - m_new)
        p = jnp.exp(s - m_new)
        l = alpha * l + jnp.sum(p, axis=-1, keepdims=True)
        acc = alpha * acc + _dot(p.astype(BF16), vb)
        return m_new, l, acc

    init = (jnp.full((rows, 1), -jnp.inf, F32), jnp.zeros((rows, 1), F32), jnp.zeros((rows, hd), F32))
    m, l, acc = lax.fori_loop(0, n_kv_blocks, body, init)
    o = acc / l
    for g in range(group):
        o_ref[0, :, g * hd:(g + 1) * hd] = o[g * tq:(g + 1) * tq].astype(o_ref.dtype)


def _flash(q, k, v, n_ctx_rows):
    b, s, n_q = q.shape
    kvh = k.shape[1]
    group = n_q // (kvh * ATTN_HEAD_DIM)
    gw = group * ATTN_HEAD_DIM
    n_lat = s - n_ctx_rows
    off = n_ctx_rows // ATTN_TQ
    return pl.pallas_call(
        functools.partial(_flash_kernel, group=group, n_kv_blocks=s // ATTN_BK),
        grid=(b, kvh, n_lat // ATTN_TQ),
        in_specs=[pl.BlockSpec((1, ATTN_TQ, gw), lambda i, h, j: (i, j + off, h)),
                  pl.BlockSpec((1, 1, s, ATTN_HEAD_DIM), lambda i, h, j: (i, h, 0, 0)),
                  pl.BlockSpec((1, 1, s, ATTN_HEAD_DIM), lambda i, h, j: (i, h, 0, 0))],
        out_specs=pl.BlockSpec((1, ATTN_TQ, gw), lambda i, h, j: (i, j, h)),
        out_shape=jax.ShapeDtypeStruct((b, n_lat, n_q), BF16),
        compiler_params=_params(3),
        name="flash_attention",
    )(q, k, v)


def _attn_out_kernel(o_ref, w_ref, x_ref, m_ref, y_ref):
    y_ref[0] = x_ref[0] + m_ref[0, 2:3, :] * _dot(o_ref[0], w_ref[...])


def _attn_out(o, out_w, xall, mod, n_ctx_tiles):
    b, n_lat, n_q = o.shape
    d = xall.shape[2]
    return pl.pallas_call(
        _attn_out_kernel,
        grid=(b, n_lat // ROW_TILE),
        in_specs=[pl.BlockSpec((1, ROW_TILE, n_q), lambda i, j: (i, j, 0)),
                  pl.BlockSpec((n_q, d), lambda i, j: (0, 0)),
                  pl.BlockSpec((1, ROW_TILE, d), lambda i, j: (i, j + n_ctx_tiles, 0)),
                  pl.BlockSpec((1, N_MOD, d), lambda i, j: (i, 0, 0))],
        out_specs=pl.BlockSpec((1, ROW_TILE, d), lambda i, j: (i, j, 0)),
        out_shape=jax.ShapeDtypeStruct((b, n_lat, d), F32),
        compiler_params=_params(2),
        name="attn_out",
    )(o, out_w, xall, mod)


def _router_kernel(x_ref, g_ref, m_ref, wt_ref, rb_ref, upper_ref, f_ref, idx_ref, gate_ref, rank_ref, cnt_ref,
                   carry_ref, *, n_experts):
    first = jnp.logical_and(pl.program_id(0) == 0, pl.program_id(1) == 0)

    @pl.when(first)
    def _():
        carry_ref[...] = jnp.zeros_like(carry_ref)

    f = _rms(x_ref[0], g_ref[...]) * (1.0 + m_ref[0, 4:5, :]) + m_ref[0, 3:4, :]
    f_ref[0] = f
    logits = lax.dot_general(wt_ref[...], f, NT_DIMS, preferred_element_type=F32,
                             precision=lax.Precision.HIGHEST) + rb_ref[...]
    rows = logits.shape[1]
    eio = lax.broadcasted_iota(I32, (n_experts, rows), 0).astype(F32)
    vals, idxs, hots = [], [], []
    cur = logits
    for _ in range(TOP_K):
        mx = jnp.max(cur, axis=0, keepdims=True)
        ix = jnp.min(jnp.where(cur == mx, eio, float(n_experts)), axis=0, keepdims=True)
        hot = eio == ix
        vals.append(mx)
        idxs.append(ix)
        hots.append(hot)
        cur = jnp.where(hot, -jnp.inf, cur)
    ex = [jnp.exp(v - vals[0]) for v in vals]
    den = ex[0]
    for e in ex[1:]:
        den = den + e
    gate_ref[...] = jnp.concatenate([e / den for e in ex], axis=0)
    idx_ref[...] = jnp.concatenate(idxs, axis=0).astype(I32)
    chosen = hots[0].astype(F32)
    for hot in hots[1:]:
        chosen = chosen + hot.astype(F32)
    before = _dot(chosen.astype(BF16), upper_ref[...])
    base = carry_ref[...] + before
    rank_ref[...] = jnp.concatenate(
        [jnp.sum(jnp.where(hot, base, 0.0), axis=0, keepdims=True) for hot in hots], axis=0).astype(I32)
    carry_ref[...] = carry_ref[...] + jnp.sum(chosen, axis=1, keepdims=True)
    cnt_ref[...] = jnp.broadcast_to(carry_ref[...], cnt_ref.shape)


def _router(xall, g, mod, router_wt, router_b, upper, n_ctx_tiles, ctx_row):
    b, s, d = xall.shape
    e = router_wt.shape[0]
    t = b * s
    nt = s // ROW_TILE
    tok = lambda i, j: (0, i * nt + j)
    return pl.pallas_call(
        functools.partial(_router_kernel, n_experts=e),
        grid=(b, nt),
        in_specs=[pl.BlockSpec((1, ROW_TILE, d), lambda i, j: (i, j, 0)),
                  pl.BlockSpec((1, d), lambda i, j: (0, 0)),
                  pl.BlockSpec((1, N_MOD, d), _mod_spec(n_ctx_tiles, ctx_row)),
                  pl.BlockSpec((e, d), lambda i, j: (0, 0)),
                  pl.BlockSpec((e, 1), lambda i, j: (0, 0)),
                  pl.BlockSpec((ROW_TILE, ROW_TILE), lambda i, j: (0, 0))],
        out_specs=[pl.BlockSpec((1, ROW_TILE, d), lambda i, j: (i, j, 0)),
                   pl.BlockSpec((TOP_K, ROW_TILE), tok),
                   pl.BlockSpec((TOP_K, ROW_TILE), tok),
                   pl.BlockSpec((TOP_K, ROW_TILE), tok),
                   pl.BlockSpec((e, LANES), lambda i, j: (0, 0))],
        out_shape=[jax.ShapeDtypeStruct((b, s, d), F32),
                   jax.ShapeDtypeStruct((TOP_K, t), I32),
                   jax.ShapeDtypeStruct((TOP_K, t), F32),
                   jax.ShapeDtypeStruct((TOP_K, t), I32),
                   jax.ShapeDtypeStruct((e, LANES), F32)],
        scratch_shapes=[pltpu.VMEM((e, 1), F32)],
        compiler_params=_params(2),
        name="moe_router",
    )(xall, g.reshape(1, d), mod, router_wt, router_b.reshape(e, 1), upper)


def _gather_rows(idx_ref, n, src_hbm, dst, sem):
    def body(r, carry):
        pltpu.make_async_copy(src_hbm.at[pl.ds(idx_ref[0, 0, r], 1), :], dst.at[pl.ds(r, 1), :], sem).start()
        return carry
    lax.fori_loop(0, n, body, 0)


def _wait_rows(n, src_hbm, dst, sem):
    pltpu.make_async_copy(src_hbm.at[pl.ds(0, n), :], dst, sem).wait()


def _expert_kernel(bexp_ref, nused_ref, tok_ref, tok_next_ref, f_hbm, wgu_ref, bgu_ref, wdn_ref, bdn_ref, o_ref,
                   xbuf, sem):
    b = pl.program_id(0)
    nused = nused_ref[0]
    slot = b % 2
    de = wdn_ref.shape[1]

    @pl.when(b == 0)
    def _():
        _gather_rows(tok_ref, MOE_BLOCK, f_hbm, xbuf.at[0], sem.at[0])

    @pl.when(b + 1 < nused)
    def _():
        _gather_rows(tok_next_ref, MOE_BLOCK, f_hbm, xbuf.at[1 - slot], sem.at[1 - slot])

    @pl.when(b < nused)
    def _():
        _wait_rows(MOE_BLOCK, f_hbm, xbuf.at[slot], sem.at[slot])
        x = xbuf[slot].astype(BF16)
        gu = _dot(x, wgu_ref[0]) + bgu_ref[0]
        g = jnp.minimum(gu[:, :de], SWIGLU_LIMIT)
        u = jnp.clip(gu[:, de:], -SWIGLU_LIMIT, SWIGLU_LIMIT)
        a = (g * _sigmoid(SWIGLU_ALPHA * g) * (u + 1.0)).astype(BF16)
        o_ref[...] = _dot(a, wdn_ref[0]) + bdn_ref[0]

    @pl.when(b >= nused)
    def _():
        o_ref[...] = jnp.zeros_like(o_ref)


def _experts(f, row_tok, block_exp, nused, w_gu, b_gu, w_dn, b_dn):
    t, d = f.shape
    n_blocks = row_tok.shape[0]
    e, _, de2 = w_gu.shape
    de = de2 // 2
    last = n_blocks - 1
    grid_spec = pltpu.PrefetchScalarGridSpec(
        num_scalar_prefetch=2,
        grid=(n_blocks,),
        in_specs=[pl.BlockSpec((1, 1, MOE_BLOCK), lambda i, be, nu: (i, 0, 0), memory_space=pltpu.SMEM),
                  pl.BlockSpec((1, 1, MOE_BLOCK), lambda i, be, nu: (jnp.minimum(i + 1, last), 0, 0),
                               memory_space=pltpu.SMEM),
                  pl.BlockSpec(memory_space=pl.ANY),
                  pl.BlockSpec((1, d, de2), lambda i, be, nu: (be[i], 0, 0)),
                  pl.BlockSpec((1, 1, de2), lambda i, be, nu: (be[i], 0, 0)),
                  pl.BlockSpec((1, de, d), lambda i, be, nu: (be[i], 0, 0)),
                  pl.BlockSpec((1, 1, d), lambda i, be, nu: (be[i], 0, 0))],
        out_specs=pl.BlockSpec((MOE_BLOCK, d), lambda i, be, nu: (i, 0)),
        scratch_shapes=[pltpu.VMEM((2, MOE_BLOCK, d), F32), pltpu.SemaphoreType.DMA((2,))],
    )
    return pl.pallas_call(
        _expert_kernel,
        grid_spec=grid_spec,
        out_shape=jax.ShapeDtypeStruct((n_blocks * MOE_BLOCK, d), F32),
        compiler_params=_params(1),
        name="moe_experts",
    )(block_exp, nused, row_tok, row_tok, f, w_gu, b_gu.reshape(e, 1, de2), w_dn, b_dn.reshape(e, 1, d))


def _combine_kernel(dest_ref, dest_next_ref, y_hbm, gate_ref, x_ref, m_ref, fg_ref, o_ref, buf, sem, *,
                    n_tiles, final_norm):
    i = pl.program_id(0) * pl.num_programs(1) + pl.program_id(1)
    slot = i % 2
    n = TOP_K * ROW_TILE
    d = x_ref.shape[2]

    @pl.when(i == 0)
    def _():
        _gather_rows(dest_ref, n, y_hbm, buf.at[0], sem.at[0])

    @pl.when(i + 1 < n_tiles)
    def _():
        _gather_rows(dest_next_ref, n, y_hbm, buf.at[1 - slot], sem.at[1 - slot])

    _wait_rows(n, y_hbm, buf.at[slot], sem.at[slot])
    acc = jnp.zeros((ROW_TILE, d), F32)
    for k in range(TOP_K):
        acc = acc + gate_ref[:, k:k + 1] * buf[slot, k * ROW_TILE:(k + 1) * ROW_TILE, :]
    out = x_ref[0] + m_ref[0, 5:6, :] * acc
    if final_norm:
        out = _rms(out, fg_ref[...])
    o_ref[0] = out


def _combine(y_rows, dest_tiles, gate_t, xall, mod, final_g, n_ctx_tiles, ctx_row, final_norm):
    b, s, d = xall.shape
    nt = s // ROW_TILE
    n_tiles = b * nt
    n = TOP_K * ROW_TILE
    last = n_tiles - 1
    return pl.pallas_call(
        functools.partial(_combine_kernel, n_tiles=n_tiles, final_norm=final_norm),
        grid=(b, nt),
        in_specs=[pl.BlockSpec((1, 1, n), lambda i, j: (i * nt + j, 0, 0), memory_space=pltpu.SMEM),
                  pl.BlockSpec((1, 1, n), lambda i, j: (jnp.minimum(i * nt + j + 1, last), 0, 0),
                               memory_space=pltpu.SMEM),
                  pl.BlockSpec(memory_space=pl.ANY),
                  pl.BlockSpec((ROW_TILE, TOP_K), lambda i, j: (i * nt + j, 0)),
                  pl.BlockSpec((1, ROW_TILE, d), lambda i, j: (i, j, 0)),
                  pl.BlockSpec((1, N_MOD, d), _mod_spec(n_ctx_tiles, ctx_row)),
                  pl.BlockSpec((1, d), lambda i, j: (0, 0))],
        out_specs=pl.BlockSpec((1, ROW_TILE, d), lambda i, j: (i, j, 0)),
        out_shape=jax.ShapeDtypeStruct((b, s, d), F32),
        scratch_shapes=[pltpu.VMEM((2, n, d), F32), pltpu.SemaphoreType.DMA((2,))],
        compiler_params=_params(2),
        name="moe_combine",
    )(dest_tiles, dest_tiles, y_rows, gate_t, xall, mod, final_g.reshape(1, d))


def _moe(xall, norm_g, mod, router_w, router_b, w_gu, b_gu, w_dn, b_dn, upper, final_g, n_ctx_tiles, ctx_row,
         final_norm):
    b, s, d = xall.shape
    t = b * s
    e = router_w.shape[1]
    f, idx, gate, rank, cnt = _router(xall, norm_g, mod, router_w.T, router_b, upper, n_ctx_tiles, ctx_row)
    counts = cnt[:, 0].astype(I32)
    padded = (counts + MOE_BLOCK - 1) // MOE_BLOCK * MOE_BLOCK
    pad_end = jnp.cumsum(padded)
    pad_start = pad_end - padded
    dest = pad_start[idx] + rank
    n_blocks = -(-(t * TOP_K) // MOE_BLOCK) + e
    tok = jnp.broadcast_to(jnp.arange(t, dtype=I32)[None, :], (TOP_K, t))
    row_tok = jnp.zeros((n_blocks * MOE_BLOCK,), I32).at[dest.reshape(-1)].set(tok.reshape(-1))
    nused = (pad_end[-1:] // MOE_BLOCK).astype(I32)
    block_exp = jnp.minimum(jnp.searchsorted(pad_end, jnp.arange(n_blocks, dtype=I32) * MOE_BLOCK, side="right"),
                            e - 1).astype(I32)
    y_rows = _experts(f.reshape(t, d), row_tok.reshape(n_blocks, 1, MOE_BLOCK), block_exp, nused,
                      w_gu, b_gu, w_dn, b_dn)
    n_tiles = t // ROW_TILE
    dest_tiles = dest.reshape(TOP_K, n_tiles, ROW_TILE).transpose(1, 0, 2).reshape(n_tiles, 1, TOP_K * ROW_TILE)
    return _combine(y_rows, dest_tiles, gate.T, xall, mod, final_g, n_ctx_tiles, ctx_row, final_norm)


def kernel(x, c, ctx, c_ctx, mod_w, mod_b, norm1_g, norm2_g, final_norm_g, ssd_in_w, ssd_conv_w, ssd_conv_b,
           ssd_dt_bias, ssd_a_log, ssd_d, ssd_norm_g, ssd_out_w, attn_qkv_w, attn_q_norm_g, attn_k_norm_g,
           attn_out_w, moe_router_w, moe_router_b, moe_w_gate_up, moe_b_gate_up, moe_w_down, moe_b_down):
    bsz, seq, d = x.shape
    n_ctx = ctx.shape[1]
    depth = mod_w.shape[0]
    assert n_ctx % ROW_TILE == 0 and seq % ROW_TILE == 0 and depth == 2
    s = n_ctx + seq
    n_ctx_tiles = n_ctx // ROW_TILE
    ctx_row = bsz

    rows = -(-(bsz + 1) // SUBLANES) * SUBLANES
    c_pad = jnp.zeros((rows, d), F32).at[:bsz].set(c).at[bsz].set(c_ctx)
    mod = _modulation(c_pad, mod_w, mod_b).reshape(depth, rows, N_MOD, d)

    xall = jnp.concatenate([ctx, x], axis=1)
    upper = (jnp.arange(ROW_TILE)[:, None] < jnp.arange(ROW_TILE)[None, :]).astype(BF16)

    d_inner = ssd_out_w.shape[1]
    heads = d_inner // SSD_HEAD_DIM
    hpg = heads // SSD_GROUPS
    gn = SSD_GROUPS * SSD_STATE
    conv_ch = d_inner + 2 * gn
    in_w = ssd_in_w[0]
    w_z = in_w[:, :2 * d_inner].astype(BF16)
    w_xbc = in_w[:, 2 * d_inner:2 * d_inner + conv_ch].astype(BF16)
    w_dt = jnp.zeros((d, LANES), F32).at[:, :2 * heads].set(in_w[:, 2 * d_inner + conv_ch:]).astype(BF16)

    h = _norm_mod(xall, norm1_g[0], mod[0], n_ctx_tiles, ctx_row, shift=0, scale=1)
    h2 = h.reshape(bsz * s, d)
    z = _matmul(h2, w_z, BF16, "ssd_in_z").reshape(bsz, s, 2 * d_inner)
    xbc = _matmul(h2, w_xbc, F32, "ssd_in_xbc").reshape(bsz, s, conv_ch)
    dtraw = _matmul(h2, w_dt, F32, "ssd_in_dt").reshape(bsz, s, LANES)
    xc = _conv_silu(xbc, ssd_conv_w[0], ssd_conv_b[0], n_ctx_tiles)

    src = jnp.arange(LANES)[None, None, :, None]
    dst = jnp.arange(LANES)[None, None, None, :]
    dd = jnp.arange(2)[:, None, None, None]
    gg = jnp.arange(SSD_GROUPS)[None, :, None, None]
    sel = jnp.logical_and(src == dd * heads + gg * hpg + dst, dst < hpg).astype(BF16).reshape(2 * SSD_GROUPS, LANES, LANES)

    def per_group(p):
        return jnp.zeros((2, SSD_GROUPS, LANES), F32).at[:, :, :hpg].set(
            p.reshape(2, SSD_GROUPS, hpg)).reshape(2 * SSD_GROUPS, 1, LANES)

    bias_g = per_group(ssd_dt_bias[0])
    a_g = per_group(-jnp.exp(ssd_a_log[0].astype(F32)))
    tri_f = (jnp.arange(SSD_CHUNK)[:, None] >= jnp.arange(SSD_CHUNK)[None, :]).astype(BF16)
    n_ctx_chunks = n_ctx // SSD_CHUNK
    yf = _ssd_scan(xc, dtraw, sel, bias_g, a_g, tri_f, n_ctx_chunks, False, d_inner)
    yb = _ssd_scan(xc, dtraw, sel, bias_g, a_g, tri_f.T, n_ctx_chunks, True, d_inner)
    dskip = jnp.repeat(ssd_d[0], SSD_HEAD_DIM, axis=1)
    xall = _ssd_out(yf, yb, z, xc, dskip, ssd_norm_g[0], ssd_out_w[0].astype(BF16), xall, mod[0],
                    n_ctx_tiles, ctx_row)
    xall = _moe(xall, norm2_g[0], mod[0], moe_router_w[0], moe_router_b[0], moe_w_gate_up[0].astype(BF16),
                moe_b_gate_up[0], moe_w_down[0].astype(BF16), moe_b_down[0], upper, final_norm_g,
                n_ctx_tiles, ctx_row, False)

    n_q = attn_out_w.shape[1]
    n_kv = ATTN_KV_HEADS * ATTN_HEAD_DIM
    pairs = ATTN_HEAD_DIM // 4
    pos = jnp.arange(seq, dtype=I32)
    inv_freq = jnp.power(ROPE_THETA, -jnp.arange(pairs, dtype=F32) / pairs)
    ang = jnp.concatenate([(pos // GRID_W).astype(F32)[:, None] * inv_freq,
                           (pos % GRID_W).astype(F32)[:, None] * inv_freq], axis=-1)
    ang = jnp.concatenate([jnp.zeros((n_ctx, ATTN_HEAD_DIM // 2), F32), ang], axis=0)
    cos_h, sin_h = jnp.cos(ang), jnp.sin(ang)
    cos_t = jnp.tile(jnp.concatenate([cos_h, cos_h], axis=-1), (1, LANES // ATTN_HEAD_DIM))[None]
    sin_t = jnp.tile(jnp.concatenate([-sin_h, sin_h], axis=-1), (1, LANES // ATTN_HEAD_DIM))[None]
    blk = 2 * LANES
    ones_blk = (jnp.arange(blk)[:, None] // ATTN_HEAD_DIM == jnp.arange(blk)[None, :] // ATTN_HEAD_DIM).astype(BF16)
    qg = jnp.tile(attn_q_norm_g[0], blk // ATTN_HEAD_DIM)[None]
    kg = jnp.tile(attn_k_norm_g[0], blk // ATTN_HEAD_DIM)[None]

    h = _norm_mod(xall, norm1_g[1], mod[1], n_ctx_tiles, ctx_row, shift=0, scale=1)
    q, k, v = _qkv(h, attn_qkv_w[0].astype(BF16), ones_blk, qg, kg, cos_t, sin_t, n_q, n_kv)
    o = _flash(q, k, v, n_ctx)
    lat = _attn_out(o, attn_out_w[0].astype(BF16), xall, mod[1], n_ctx_tiles)
    return _moe(lat, norm2_g[1], mod[1], moe_router_w[1], moe_router_b[1], moe_w_gate_up[1].astype(BF16),
                moe_b_gate_up[1], moe_w_down[1].astype(BF16), moe_b_down[1], upper, final_norm_g,
                0, ctx_row, True)
```

```python
import functools
import math

import jax
import jax.numpy as jnp
from jax import lax
from jax.experimental import pallas as pl
from jax.experimental.pallas import tpu as pltpu

F32 = jnp.float32
BF16 = jnp.bfloat16
I32 = jnp.int32

GRID_W = 64
SSD_HEAD_DIM = 64
SSD_GROUPS = 4
SSD_STATE = 128
SSD_CONV = 5
SSD_CHUNK = 128
ATTN_HEAD_DIM = 64
ATTN_KV_HEADS = 4
ROPE_THETA = 10000.0
TOP_K = 4
SWIGLU_ALPHA = 1.702
SWIGLU_LIMIT = 7.0
N_MOD = 6
NORM_EPS = 1e-6
LOG2E = math.log2(math.e)

LANES = 128
SUBLANES = 8
VMEM_LIMIT = 48 * 1024 * 1024

ROW_TILE = 256
MOE_BLOCK = 256
ATTN_TQ = 256
ATTN_BK = 768
SSD_GROUPS_PER_STEP = 4

NT_DIMS = (((1,), (1,)), ((), ()))


def _params(n_grid, vmem=VMEM_LIMIT):
    return pltpu.CompilerParams(dimension_semantics=("arbitrary",) * n_grid, vmem_limit_bytes=vmem)


def _split3(x):
    h = x.astype(BF16)
    r = x - h.astype(F32)
    m = r.astype(BF16)
    l = (r - m.astype(F32)).astype(BF16)
    return h, m, l


def _dot(a, b):
    return jnp.dot(a, b, preferred_element_type=F32)


def _dot_f32_by_exact(x, w):
    h, m, l = _split3(x)
    return _dot(h, w) + _dot(m, w) + _dot(l, w)


def _exact_by_dot_f32(w, x):
    h, m, l = _split3(x)
    return _dot(w, h) + _dot(w, m) + _dot(w, l)


def _sigmoid(x):
    return 1.0 / (1.0 + jnp.exp(-x))


def _rms(x, g):
    ms = jnp.mean(x * x, axis=-1, keepdims=True)
    return x * lax.rsqrt(ms + NORM_EPS) * g


def _mod_kernel(c_ref, w_ref, b_ref, o_ref):
    c = c_ref[...]
    s = c * _sigmoid(c)
    o_ref[0] = jnp.dot(s, w_ref[0], preferred_element_type=F32, precision=lax.Precision.HIGHEST) + b_ref[0]


def _modulation(c_pad, mod_w, mod_b):
    depth, d, n = mod_w.shape
    rows = c_pad.shape[0]
    tn = 1024
    return pl.pallas_call(
        _mod_kernel,
        grid=(depth, n // tn),
        in_specs=[pl.BlockSpec((rows, d), lambda i, j: (0, 0)),
                  pl.BlockSpec((1, d, tn), lambda i, j: (i, 0, j)),
                  pl.BlockSpec((1, 1, tn), lambda i, j: (i, 0, j))],
        out_specs=pl.BlockSpec((1, rows, tn), lambda i, j: (i, 0, j)),
        out_shape=jax.ShapeDtypeStruct((depth, rows, n), F32),
        compiler_params=_params(2),
        name="modulation",
    )(c_pad, mod_w, mod_b.reshape(depth, 1, n))


def _mod_spec(n_ctx_tiles, ctx_row):
    def index(b, j):
        return (jnp.where(j < n_ctx_tiles, ctx_row, b), 0, 0)
    return index


def _norm_mod_kernel(x_ref, g_ref, m_ref, o_ref, *, shift, scale):
    y = _rms(x_ref[0], g_ref[...])
    o_ref[0] = (y * (1.0 + m_ref[0, scale:scale + 1, :]) + m_ref[0, shift:shift + 1, :]).astype(o_ref.dtype)


def _norm_mod(xall, g, mod, n_ctx_tiles, ctx_row, shift, scale):
    b, s, d = xall.shape
    return pl.pallas_call(
        functools.partial(_norm_mod_kernel, shift=shift, scale=scale),
        grid=(b, s // ROW_TILE),
        in_specs=[pl.BlockSpec((1, ROW_TILE, d), lambda i, j: (i, j, 0)),
                  pl.BlockSpec((1, d), lambda i, j: (0, 0)),
                  pl.BlockSpec((1, N_MOD, d), _mod_spec(n_ctx_tiles, ctx_row))],
        out_specs=pl.BlockSpec((1, ROW_TILE, d), lambda i, j: (i, j, 0)),
        out_shape=jax.ShapeDtypeStruct((b, s, d), BF16),
        compiler_params=_params(2),
        name="norm_mod",
    )(xall, g.reshape(1, d), mod)


def _mm_kernel(x_ref, w_ref, o_ref):
    o_ref[...] = _dot(x_ref[...], w_ref[...]).astype(o_ref.dtype)


def _matmul(x, w, out_dtype, name, tm=512, tn=1024):
    m, k = x.shape
    n = w.shape[1]
    tn = min(tn, n)
    return pl.pallas_call(
        _mm_kernel,
        grid=(n // tn, m // tm),
        in_specs=[pl.BlockSpec((tm, k), lambda c, r: (r, 0)),
                  pl.BlockSpec((k, tn), lambda c, r: (0, c))],
        out_specs=pl.BlockSpec((tm, tn), lambda c, r: (r, c)),
        out_shape=jax.ShapeDtypeStruct((m, n), out_dtype),
        compiler_params=_params(2),
        name=name,
    )(x, w)


def _conv_kernel(prev_ref, cur_ref, next_ref, w_ref, b_ref, o_ref, ext_ref, *, n_ctx_tiles, n_tiles):
    j = pl.program_id(1)
    pad = SSD_CONV // 2
    first = jnp.logical_or(j == 0, j == n_ctx_tiles)
    last = jnp.logical_or(j == n_ctx_tiles - 1, j == n_tiles - 1)
    ext_ref[0:SUBLANES, :] = jnp.where(first, 0.0, prev_ref[0])
    ext_ref[SUBLANES:SUBLANES + ROW_TILE, :] = cur_ref[0]
    ext_ref[SUBLANES + ROW_TILE:2 * SUBLANES + ROW_TILE, :] = jnp.where(last, 0.0, next_ref[0])
    acc = b_ref[...] + w_ref[0:1, :] * ext_ref[SUBLANES - pad:SUBLANES - pad + ROW_TILE, :]
    for k in range(1, SSD_CONV):
        acc = acc + w_ref[k:k + 1, :] * ext_ref[SUBLANES - pad + k:SUBLANES - pad + k + ROW_TILE, :]
    o_ref[0] = (acc * _sigmoid(acc)).astype(o_ref.dtype)


def _conv_silu(xbc, conv_w, conv_b, n_ctx_tiles):
    b, s, c = xbc.shape
    tc = 1024
    n_tiles = s // ROW_TILE
    rb = ROW_TILE // SUBLANES
    n_rb = s // SUBLANES
    return pl.pallas_call(
        functools.partial(_conv_kernel, n_ctx_tiles=n_ctx_tiles, n_tiles=n_tiles),
        grid=(b, n_tiles, c // tc),
        in_specs=[pl.BlockSpec((1, SUBLANES, tc), lambda i, j, k: (i, jnp.maximum(j * rb - 1, 0), k)),
                  pl.BlockSpec((1, ROW_TILE, tc), lambda i, j, k: (i, j, k)),
                  pl.BlockSpec((1, SUBLANES, tc), lambda i, j, k: (i, jnp.minimum((j + 1) * rb, n_rb - 1), k)),
                  pl.BlockSpec((SSD_CONV, tc), lambda i, j, k: (0, k)),
                  pl.BlockSpec((1, tc), lambda i, j, k: (0, k))],
        out_specs=pl.BlockSpec((1, ROW_TILE, tc), lambda i, j, k: (i, j, k)),
        out_shape=jax.ShapeDtypeStruct((b, s, c), BF16),
        scratch_shapes=[pltpu.VMEM((ROW_TILE + 2 * SUBLANES, tc), F32)],
        compiler_params=_params(3),
        name="conv_silu",
    )(xbc, xbc, xbc, conv_w, conv_b.reshape(1, c))


def _ssd_kernel(xs_ref, b_ref, c_ref, raw_ref, sel_ref, bias_ref, a_ref, tri_ref, y_ref, ht_ref, *,
                reverse, heads_per_group, groups_per_step):
    L = SSD_CHUNK
    P = SSD_HEAD_DIM
    N = SSD_STATE

    @pl.when(pl.program_id(2) == 0)
    def _():
        ht_ref[...] = jnp.zeros_like(ht_ref)

    li = lax.broadcasted_iota(I32, (L, L), 0)
    si = lax.broadcasted_iota(I32, (L, L), 1)
    valid = (li <= si) if reverse else (li >= si)
    end = 0 if reverse else L - 1
    ys = []
    for gi in range(groups_per_step):
        raw = _dot_f32_by_exact(raw_ref[0], sel_ref[gi])
        z = raw + bias_ref[gi]
        dt = jnp.maximum(z, 0.0) + jnp.log(1.0 + jnp.exp(-jnp.abs(z)))
        la = dt * a_ref[gi]
        cs = _exact_by_dot_f32(tri_ref[...], la)
        cs_t = cs.T
        dt_t = dt.T
        bg = b_ref[0, :, gi * N:(gi + 1) * N]
        cg = c_ref[0, :, gi * N:(gi + 1) * N]
        cb = lax.dot_general(cg, bg, NT_DIMS, preferred_element_type=F32)
        bg_t = bg.astype(F32).T
        cf = cg.astype(F32)
        for j in range(heads_per_group):
            hh = gi * heads_per_group + j
            colb = jnp.broadcast_to(cs[:, j:j + 1], (L, L))
            rowb = jnp.broadcast_to(cs_t[j:j + 1, :], (L, L))
            decay = jnp.exp(jnp.where(valid, colb - rowb, -jnp.inf))
            m = (cb * decay * dt_t[j:j + 1, :]).astype(BF16)
            ce = (cf * jnp.exp(colb)).astype(BF16)
            xj = xs_ref[0, :, hh * P:(hh + 1) * P]
            ht = ht_ref[hh]
            lhs = jnp.concatenate([m, ce], axis=1)
            rhs = jnp.concatenate([xj, ht.astype(BF16)], axis=0)
            ys.append(_dot(lhs, rhs))
            cse = cs_t[j:j + 1, end:end + 1]
            w_row = jnp.exp(cse - cs_t[j:j + 1, :]) * dt_t[j:j + 1, :]
            bw = (bg_t * w_row).astype(BF16)
            ht_ref[hh] = ht * jnp.exp(cse) + _dot(bw, xj)
    y_ref[0] = jnp.concatenate(ys, axis=1).astype(y_ref.dtype)


def _ssd_scan(xc, dtraw, sel, bias_g, a_g, tri, n_ctx_chunks, reverse, d_inner):
    b, s, _ = xc.shape
    n_chunks = s // SSD_CHUNK
    heads = d_inner // SSD_HEAD_DIM
    hpg = heads // SSD_GROUPS
    gps = SSD_GROUPS_PER_STEP
    gw = gps * hpg * SSD_HEAD_DIM
    bw = gps * SSD_STATE
    b_off = d_inner // bw
    c_off = (d_inner + SSD_GROUPS * SSD_STATE) // bw
    n_gb = SSD_GROUPS // gps
    d = 1 if reverse else 0

    def chunk(i):
        if not reverse:
            return i
        return jnp.where(i < n_ctx_chunks, n_ctx_chunks - 1 - i, n_chunks - 1 + n_ctx_chunks - i)

    return pl.pallas_call(
        functools.partial(_ssd_kernel, reverse=reverse, heads_per_group=hpg, groups_per_step=gps),
        grid=(b, n_gb, n_chunks),
        in_specs=[pl.BlockSpec((1, SSD_CHUNK, gw), lambda i, g, c: (i, chunk(c), g)),
                  pl.BlockSpec((1, SSD_CHUNK, bw), lambda i, g, c: (i, chunk(c), b_off + g)),
                  pl.BlockSpec((1, SSD_CHUNK, bw), lambda i, g, c: (i, chunk(c), c_off + g)),
                  pl.BlockSpec((1, SSD_CHUNK, LANES), lambda i, g, c: (i, chunk(c), 0)),
                  pl.BlockSpec((gps, LANES, LANES), lambda i, g, c: (d * n_gb + g, 0, 0)),
                  pl.BlockSpec((gps, 1, LANES), lambda i, g, c: (d * n_gb + g, 0, 0)),
                  pl.BlockSpec((gps, 1, LANES), lambda i, g, c: (d * n_gb + g, 0, 0)),
                  pl.BlockSpec((SSD_CHUNK, SSD_CHUNK), lambda i, g, c: (0, 0))],
        out_specs=pl.BlockSpec((1, SSD_CHUNK, gw), lambda i, g, c: (i, chunk(c), g)),
        out_shape=jax.ShapeDtypeStruct((b, s, d_inner), BF16),
        scratch_shapes=[pltpu.VMEM((gps * hpg, SSD_STATE, SSD_HEAD_DIM), F32)],
        compiler_params=_params(3),
        name="ssd_scan_bwd" if reverse else "ssd_scan_fwd",
    )(xc, xc, xc, dtraw, sel, bias_g, a_g, tri)


def _ssd_out_kernel(yf_ref, yb_ref, zf_ref, zb_ref, xs_ref, dsk_ref, ng_ref, w_ref, x_ref, m_ref, o_ref, *, groups):
    xs = xs_ref[0].astype(F32)
    zf = zf_ref[0].astype(F32)
    zb = zb_ref[0].astype(F32)
    y = ((yf_ref[0].astype(F32) + dsk_ref[0:1, :] * xs) * (zf * _sigmoid(zf))
         + (yb_ref[0].astype(F32) + dsk_ref[1:2, :] * xs) * (zb * _sigmoid(zb)))
    gw = y.shape[1] // groups
    parts = []
    for g in range(groups):
        parts.append(_rms(y[:, g * gw:(g + 1) * gw], ng_ref[:, g * gw:(g + 1) * gw]).astype(BF16))
    out = _dot(jnp.concatenate(parts, axis=1), w_ref[...])
    o_ref[0] = x_ref[0] + m_ref[0, 2:3, :] * out


def _ssd_out(yf, yb, z, xc, dskip, norm_g, out_w, xall, mod, n_ctx_tiles, ctx_row):
    b, s, d = xall.shape
    di = yf.shape[2]
    row = lambda i, j: (i, j, 0)
    return pl.pallas_call(
        functools.partial(_ssd_out_kernel, groups=SSD_GROUPS),
        grid=(b, s // ROW_TILE),
        in_specs=[pl.BlockSpec((1, ROW_TILE, di), row),
                  pl.BlockSpec((1, ROW_TILE, di), row),
                  pl.BlockSpec((1, ROW_TILE, di), row),
                  pl.BlockSpec((1, ROW_TILE, di), lambda i, j: (i, j, 1)),
                  pl.BlockSpec((1, ROW_TILE, di), row),
                  pl.BlockSpec((2, di), lambda i, j: (0, 0)),
                  pl.BlockSpec((1, di), lambda i, j: (0, 0)),
                  pl.BlockSpec((di, d), lambda i, j: (0, 0)),
                  pl.BlockSpec((1, ROW_TILE, d), row),
                  pl.BlockSpec((1, N_MOD, d), _mod_spec(n_ctx_tiles, ctx_row))],
        out_specs=pl.BlockSpec((1, ROW_TILE, d), row),
        out_shape=jax.ShapeDtypeStruct((b, s, d), F32),
        compiler_params=_params(2),
        name="ssd_out",
    )(yf, yb, z, z, xc, dskip, norm_g.reshape(1, di), out_w, xall, mod)


def _qkv_kernel(h_ref, w_ref, ones_ref, qg_ref, kg_ref, cos_ref, sin_ref, q_ref, k_ref, vt_ref, *, n_q, n_kv):
    hd = ATTN_HEAD_DIM
    t = _dot(h_ref[0], w_ref[...])
    blk = 2 * LANES
    cos = jnp.concatenate([cos_ref[0], cos_ref[0]], axis=1)
    sin = jnp.concatenate([sin_ref[0], sin_ref[0]], axis=1)
    lane = lax.broadcasted_iota(I32, (t.shape[0], blk), 1)
    first_half = (lane % hd) < (hd // 2)
    outs = []
    for c in range((n_q + n_kv) // blk):
        x = t[:, c * blk:(c + 1) * blk]
        sq = x * x
        hi = sq.astype(BF16)
        lo = (sq - hi.astype(F32)).astype(BF16)
        ss = _dot(hi, ones_ref[...]) + _dot(lo, ones_ref[...])
        g = qg_ref[...] if c * blk < n_q else kg_ref[...]
        xn = x * lax.rsqrt(ss * (1.0 / hd) + NORM_EPS) * g
        partner = jnp.where(first_half, pltpu.roll(xn, blk - hd // 2, axis=1), pltpu.roll(xn, hd // 2, axis=1))
        outs.append(xn * cos + partner * sin)
    q = jnp.concatenate(outs[:n_q // blk], axis=1) * (hd ** -0.5 * LOG2E)
    q_ref[0] = q.astype(q_ref.dtype)
    kk = jnp.concatenate(outs[n_q // blk:], axis=1)
    vv = t[:, n_q + n_kv:]
    one_col = (lax.broadcasted_iota(I32, (t.shape[0], LANES - hd), 1) == 0).astype(F32)
    for h in range(n_kv // hd):
        k_ref[0, h] = kk[:, h * hd:(h + 1) * hd].astype(k_ref.dtype)
        aug = jnp.concatenate([vv[:, h * hd:(h + 1) * hd], one_col], axis=1)
        vt_ref[0, h, 0] = aug.T.astype(vt_ref.dtype)


def _qkv(h, qkv_w, ones_blk, qg, kg, cos_t, sin_t, n_q, n_kv):
    b, s, d = h.shape
    kvh = n_kv // ATTN_HEAD_DIM
    blk = 2 * LANES
    return pl.pallas_call(
        functools.partial(_qkv_kernel, n_q=n_q, n_kv=n_kv),
        grid=(b, s // ROW_TILE),
        in_specs=[pl.BlockSpec((1, ROW_TILE, d), lambda i, j: (i, j, 0)),
                  pl.BlockSpec((d, n_q + 2 * n_kv), lambda i, j: (0, 0)),
                  pl.BlockSpec((blk, blk), lambda i, j: (0, 0)),
                  pl.BlockSpec((1, blk), lambda i, j: (0, 0)),
                  pl.BlockSpec((1, blk), lambda i, j: (0, 0)),
                  pl.BlockSpec((1, ROW_TILE, LANES), lambda i, j: (0, j, 0)),
                  pl.BlockSpec((1, ROW_TILE, LANES), lambda i, j: (0, j, 0))],
        out_specs=[pl.BlockSpec((1, ROW_TILE, n_q), lambda i, j: (i, j, 0)),
                   pl.BlockSpec((1, kvh, ROW_TILE, ATTN_HEAD_DIM), lambda i, j: (i, 0, j, 0)),
                   pl.BlockSpec((1, kvh, 1, LANES, ROW_TILE), lambda i, j: (i, 0, j, 0, 0))],
        out_shape=[jax.ShapeDtypeStruct((b, s, n_q), BF16),
                   jax.ShapeDtypeStruct((b, kvh, s, ATTN_HEAD_DIM), BF16),
                   jax.ShapeDtypeStruct((b, kvh, s // ROW_TILE, LANES, ROW_TILE), BF16)],
        compiler_params=_params(2),
        name="qkv_rope",
    )(h, qkv_w, ones_blk, qg, kg, cos_t, sin_t)


def _flash_kernel(q_ref, k_ref, vt_ref, o_ref, acc_ref, m_ref, s0_ref, s1_ref, c0_ref, c1_ref, *, group,
                  n_kv_blocks):
    hd = ATTN_HEAD_DIM
    tq = q_ref.shape[1]
    sub = ATTN_BK // ROW_TILE
    q = q_ref[0]
    qs = jnp.concatenate([q[:, g * hd:(g + 1) * hd] for g in range(group)], axis=0)
    m_ref[...] = jnp.full_like(m_ref, -jnp.inf)
    acc_ref[...] = jnp.zeros_like(acc_ref)

    def step(i_upd, su_ref, cu_ref, i_sc, ss_ref, cs_ref):
        if i_upd is not None:
            m_old = m_ref[...]
            m_new = jnp.maximum(m_old, cu_ref[...])
            acc = jnp.exp2(m_old - m_new) * acc_ref[...]
        cmax = None
        for u in range(sub):
            rows = slice(u * ROW_TILE, (u + 1) * ROW_TILE)
            if i_sc is not None:
                start = pl.multiple_of(i_sc * ATTN_BK + u * ROW_TILE, ROW_TILE)
                kb = k_ref[0, 0, pl.ds(start, ROW_TILE), :]
                s = lax.dot_general(kb, qs, NT_DIMS, preferred_element_type=F32)
                ss_ref[rows, :] = s
                cm = jnp.max(s, axis=0, keepdims=True)
                cmax = cm if cmax is None else jnp.maximum(cmax, cm)
            if i_upd is not None:
                p = jnp.exp2(su_ref[rows, :] - m_new).astype(BF16)
                acc = acc + _dot(vt_ref[0, 0, i_upd * sub + u], p)
        if i_sc is not None:
            cs_ref[...] = cmax
        if i_upd is not None:
            acc_ref[...] = acc
            m_ref[...] = m_new

    def pair(t, carry):
        i = 2 * t
        step(i, s0_ref, c0_ref, i + 1, s1_ref, c1_ref)
        step(i + 1, s1_ref, c1_ref, i + 2, s0_ref, c0_ref)
        return carry

    step(None, None, None, 0, s0_ref, c0_ref)
    lax.fori_loop(0, (n_kv_blocks - 1) // 2, pair, 0)
    if n_kv_blocks % 2 == 0:
        step(n_kv_blocks - 2, s0_ref, c0_ref, n_kv_blocks - 1, s1_ref, c1_ref)
        step(n_kv_blocks - 1, s1_ref, c1_ref, None, None, None)
    else:
        step(n_kv_blocks - 1, s0_ref, c0_ref, None, None, None)
    acc = acc_ref[...]
    o = (acc / acc[hd:hd + 1, :]).T
    for g in range(group):
        o_ref[0, :, g * hd:(g + 1) * hd] = o[g * tq:(g + 1) * tq, :hd].astype(o_ref.dtype)


def _flash(q, k, vt, n_ctx_rows):
    b, s, n_q = q.shape
    kvh = k.shape[1]
    group = n_q // (kvh * ATTN_HEAD_DIM)
    gw = group * ATTN_HEAD_DIM
    n_lat = s - n_ctx_rows
    off = n_ctx_rows // ATTN_TQ
    return pl.pallas_call(
        functools.partial(_flash_kernel, group=group, n_kv_blocks=s // ATTN_BK),
        grid=(b, kvh, n_lat // ATTN_TQ),
        in_specs=[pl.BlockSpec((1, ATTN_TQ, gw), lambda i, h, j: (i, j + off, h)),
                  pl.BlockSpec((1, 1, s, ATTN_HEAD_DIM), lambda i, h, j: (i, h, 0, 0)),
                  pl.BlockSpec((1, 1, s // ROW_TILE, LANES, ROW_TILE), lambda i, h, j: (i, h, 0, 0, 0))],
        out_specs=pl.BlockSpec((1, ATTN_TQ, gw), lambda i, h, j: (i, j, h)),
        out_shape=jax.ShapeDtypeStruct((b, n_lat, n_q), BF16),
        scratch_shapes=[pltpu.VMEM((LANES, group * ATTN_TQ), F32), pltpu.VMEM((1, group * ATTN_TQ), F32),
                        pltpu.VMEM((ATTN_BK, group * ATTN_TQ), F32), pltpu.VMEM((ATTN_BK, group * ATTN_TQ), F32),
                        pltpu.VMEM((1, group * ATTN_TQ), F32), pltpu.VMEM((1, group * ATTN_TQ), F32)],
        compiler_params=_params(3),
        name="flash_attention",
    )(q, k, vt)


def _attn_out_kernel(o_ref, w_ref, x_ref, m_ref, y_ref):
    y_ref[0] = x_ref[0] + m_ref[0, 2:3, :] * _dot(o_ref[0], w_ref[...])


def _attn_out(o, out_w, xall, mod, n_ctx_tiles):
    b, n_lat, n_q = o.shape
    d = xall.shape[2]
    return pl.pallas_call(
        _attn_out_kernel,
        grid=(b, n_lat // ROW_TILE),
        in_specs=[pl.BlockSpec((1, ROW_TILE, n_q), lambda i, j: (i, j, 0)),
                  pl.BlockSpec((n_q, d), lambda i, j: (0, 0)),
                  pl.BlockSpec((1, ROW_TILE, d), lambda i, j: (i, j + n_ctx_tiles, 0)),
                  pl.BlockSpec((1, N_MOD, d), lambda i, j: (i, 0, 0))],
        out_specs=pl.BlockSpec((1, ROW_TILE, d), lambda i, j: (i, j, 0)),
        out_shape=jax.ShapeDtypeStruct((b, n_lat, d), F32),
        compiler_params=_params(2),
        name="attn_out",
    )(o, out_w, xall, mod)


def _router_kernel(x_ref, g_ref, m_ref, wt_ref, rb_ref, upper_ref, f_ref, idx_ref, gate_ref, rank_ref, cnt_ref,
                   carry_ref, *, n_experts):
    first = jnp.logical_and(pl.program_id(0) == 0, pl.program_id(1) == 0)

    @pl.when(first)
    def _():
        carry_ref[...] = jnp.zeros_like(carry_ref)

    f = _rms(x_ref[0], g_ref[...]) * (1.0 + m_ref[0, 4:5, :]) + m_ref[0, 3:4, :]
    f_ref[0] = f
    logits = lax.dot_general(wt_ref[...], f, NT_DIMS, preferred_element_type=F32,
                             precision=lax.Precision.HIGHEST) + rb_ref[...]
    rows = logits.shape[1]
    eio = lax.broadcasted_iota(I32, (n_experts, rows), 0).astype(F32)
    vals, idxs, hots = [], [], []
    cur = logits
    for _ in range(TOP_K):
        mx = jnp.max(cur, axis=0, keepdims=True)
        ix = jnp.min(jnp.where(cur == mx, eio, float(n_experts)), axis=0, keepdims=True)
        hot = eio == ix
        vals.append(mx)
        idxs.append(ix)
        hots.append(hot)
        cur = jnp.where(hot, -jnp.inf, cur)
    ex = [jnp.exp(v - vals[0]) for v in vals]
    den = ex[0]
    for e in ex[1:]:
        den = den + e
    gate_ref[...] = jnp.concatenate([e / den for e in ex], axis=0)
    idx_ref[...] = jnp.concatenate(idxs, axis=0).astype(I32)
    chosen = hots[0].astype(F32)
    for hot in hots[1:]:
        chosen = chosen + hot.astype(F32)
    before = _dot(chosen.astype(BF16), upper_ref[...])
    base = carry_ref[...] + before
    rank_ref[...] = jnp.concatenate(
        [jnp.sum(jnp.where(hot, base, 0.0), axis=0, keepdims=True) for hot in hots], axis=0).astype(I32)
    carry_ref[...] = carry_ref[...] + jnp.sum(chosen, axis=1, keepdims=True)
    cnt_ref[...] = jnp.broadcast_to(carry_ref[...], cnt_ref.shape)


def _router(xall, g, mod, router_wt, router_b, upper, n_ctx_tiles, ctx_row):
    b, s, d = xall.shape
    e = router_wt.shape[0]
    t = b * s
    nt = s // ROW_TILE
    tok = lambda i, j: (0, i * nt + j)
    return pl.pallas_call(
        functools.partial(_router_kernel, n_experts=e),
        grid=(b, nt),
        in_specs=[pl.BlockSpec((1, ROW_TILE, d), lambda i, j: (i, j, 0)),
                  pl.BlockSpec((1, d), lambda i, j: (0, 0)),
                  pl.BlockSpec((1, N_MOD, d), _mod_spec(n_ctx_tiles, ctx_row)),
                  pl.BlockSpec((e, d), lambda i, j: (0, 0)),
                  pl.BlockSpec((e, 1), lambda i, j: (0, 0)),
                  pl.BlockSpec((ROW_TILE, ROW_TILE), lambda i, j: (0, 0))],
        out_specs=[pl.BlockSpec((1, ROW_TILE, d), lambda i, j: (i, j, 0)),
                   pl.BlockSpec((TOP_K, ROW_TILE), tok),
                   pl.BlockSpec((TOP_K, ROW_TILE), tok),
                   pl.BlockSpec((TOP_K, ROW_TILE), tok),
                   pl.BlockSpec((e, LANES), lambda i, j: (0, 0))],
        out_shape=[jax.ShapeDtypeStruct((b, s, d), F32),
                   jax.ShapeDtypeStruct((TOP_K, t), I32),
                   jax.ShapeDtypeStruct((TOP_K, t), F32),
                   jax.ShapeDtypeStruct((TOP_K, t), I32),
                   jax.ShapeDtypeStruct((e, LANES), F32)],
        scratch_shapes=[pltpu.VMEM((e, 1), F32)],
        compiler_params=_params(2),
        name="moe_router",
    )(xall, g.reshape(1, d), mod, router_wt, router_b.reshape(e, 1), upper)


def _gather_rows(idx_ref, n, src_hbm, dst, sem):
    def body(r, carry):
        pltpu.make_async_copy(src_hbm.at[pl.ds(idx_ref[0, 0, r], 1), :], dst.at[pl.ds(r, 1), :], sem).start()
        return carry
    lax.fori_loop(0, n, body, 0)


def _wait_rows(n, src_hbm, dst, sem):
    pltpu.make_async_copy(src_hbm.at[pl.ds(0, n), :], dst, sem).wait()


def _expert_kernel(bexp_ref, nused_ref, tok_ref, tok_next_ref, f_hbm, wgu_ref, bgu_ref, wdn_ref, bdn_ref, o_ref,
                   xbuf, sem):
    b = pl.program_id(0)
    nused = nused_ref[0]
    slot = b % 2
    de = wdn_ref.shape[1]

    @pl.when(b == 0)
    def _():
        _gather_rows(tok_ref, MOE_BLOCK, f_hbm, xbuf.at[0], sem.at[0])

    @pl.when(b + 1 < nused)
    def _():
        _gather_rows(tok_next_ref, MOE_BLOCK, f_hbm, xbuf.at[1 - slot], sem.at[1 - slot])

    @pl.when(b < nused)
    def _():
        _wait_rows(MOE_BLOCK, f_hbm, xbuf.at[slot], sem.at[slot])
        x = xbuf[slot].astype(BF16)
        gu = _dot(x, wgu_ref[0]) + bgu_ref[0]
        g = jnp.minimum(gu[:, :de], SWIGLU_LIMIT)
        u = jnp.clip(gu[:, de:], -SWIGLU_LIMIT, SWIGLU_LIMIT)
        a = (g * _sigmoid(SWIGLU_ALPHA * g) * (u + 1.0)).astype(BF16)
        o_ref[...] = _dot(a, wdn_ref[0]) + bdn_ref[0]

    @pl.when(b >= nused)
    def _():
        o_ref[...] = jnp.zeros_like(o_ref)


def _experts(f, row_tok, block_exp, nused, w_gu, b_gu, w_dn, b_dn):
    t, d = f.shape
    n_blocks = row_tok.shape[0]
    e, _, de2 = w_gu.shape
    de = de2 // 2
    last = n_blocks - 1
    grid_spec = pltpu.PrefetchScalarGridSpec(
        num_scalar_prefetch=2,
        grid=(n_blocks,),
        in_specs=[pl.BlockSpec((1, 1, MOE_BLOCK), lambda i, be, nu: (i, 0, 0), memory_space=pltpu.SMEM),
                  pl.BlockSpec((1, 1, MOE_BLOCK), lambda i, be, nu: (jnp.minimum(i + 1, last), 0, 0),
                               memory_space=pltpu.SMEM),
                  pl.BlockSpec(memory_space=pl.ANY),
                  pl.BlockSpec((1, d, de2), lambda i, be, nu: (be[i], 0, 0)),
                  pl.BlockSpec((1, 1, de2), lambda i, be, nu: (be[i], 0, 0)),
                  pl.BlockSpec((1, de, d), lambda i, be, nu: (be[i], 0, 0)),
                  pl.BlockSpec((1, 1, d), lambda i, be, nu: (be[i], 0, 0))],
        out_specs=pl.BlockSpec((MOE_BLOCK, d), lambda i, be, nu: (i, 0)),
        scratch_shapes=[pltpu.VMEM((2, MOE_BLOCK, d), F32), pltpu.SemaphoreType.DMA((2,))],
    )
    return pl.pallas_call(
        _expert_kernel,
        grid_spec=grid_spec,
        out_shape=jax.ShapeDtypeStruct((n_blocks * MOE_BLOCK, d), F32),
        compiler_params=_params(1),
        name="moe_experts",
    )(block_exp, nused, row_tok, row_tok, f, w_gu, b_gu.reshape(e, 1, de2), w_dn, b_dn.reshape(e, 1, d))


def _combine_kernel(dest_ref, dest_next_ref, y_hbm, gate_ref, x_ref, m_ref, fg_ref, o_ref, buf, sem, *,
                    n_tiles, final_norm):
    i = pl.program_id(0) * pl.num_programs(1) + pl.program_id(1)
    slot = i % 2
    n = TOP_K * ROW_TILE
    d = x_ref.shape[2]

    @pl.when(i == 0)
    def _():
        _gather_rows(dest_ref, n, y_hbm, buf.at[0], sem.at[0])

    @pl.when(i + 1 < n_tiles)
    def _():
        _gather_rows(dest_next_ref, n, y_hbm, buf.at[1 - slot], sem.at[1 - slot])

    _wait_rows(n, y_hbm, buf.at[slot], sem.at[slot])
    acc = jnp.zeros((ROW_TILE, d), F32)
    for k in range(TOP_K):
        acc = acc + gate_ref[:, k:k + 1] * buf[slot, k * ROW_TILE:(k + 1) * ROW_TILE, :]
    out = x_ref[0] + m_ref[0, 5:6, :] * acc
    if final_norm:
        out = _rms(out, fg_ref[...])
    o_ref[0] = out


def _combine(y_rows, dest_tiles, gate_t, xall, mod, final_g, n_ctx_tiles, ctx_row, final_norm):
    b, s, d = xall.shape
    nt = s // ROW_TILE
    n_tiles = b * nt
    n = TOP_K * ROW_TILE
    last = n_tiles - 1
    return pl.pallas_call(
        functools.partial(_combine_kernel, n_tiles=n_tiles, final_norm=final_norm),
        grid=(b, nt),
        in_specs=[pl.BlockSpec((1, 1, n), lambda i, j: (i * nt + j, 0, 0), memory_space=pltpu.SMEM),
                  pl.BlockSpec((1, 1, n), lambda i, j: (jnp.minimum(i * nt + j + 1, last), 0, 0),
                               memory_space=pltpu.SMEM),
                  pl.BlockSpec(memory_space=pl.ANY),
                  pl.BlockSpec((ROW_TILE, TOP_K), lambda i, j: (i * nt + j, 0)),
                  pl.BlockSpec((1, ROW_TILE, d), lambda i, j: (i, j, 0)),
                  pl.BlockSpec((1, N_MOD, d), _mod_spec(n_ctx_tiles, ctx_row)),
                  pl.BlockSpec((1, d), lambda i, j: (0, 0))],
        out_specs=pl.BlockSpec((1, ROW_TILE, d), lambda i, j: (i, j, 0)),
        out_shape=jax.ShapeDtypeStruct((b, s, d), F32),
        scratch_shapes=[pltpu.VMEM((2, n, d), F32), pltpu.SemaphoreType.DMA((2,))],
        compiler_params=_params(2),
        name="moe_combine",
    )(dest_tiles, dest_tiles, y_rows, gate_t, xall, mod, final_g.reshape(1, d))


def _moe(xall, norm_g, mod, router_w, router_b, w_gu, b_gu, w_dn, b_dn, upper, final_g, n_ctx_tiles, ctx_row,
         final_norm):
    b, s, d = xall.shape
    t = b * s
    e = router_w.shape[1]
    f, idx, gate, rank, cnt = _router(xall, norm_g, mod, router_w.T, router_b, upper, n_ctx_tiles, ctx_row)
    counts = cnt[:, 0].astype(I32)
    padded = (counts + MOE_BLOCK - 1) // MOE_BLOCK * MOE_BLOCK
    pad_end = jnp.cumsum(padded)
    pad_start = pad_end - padded
    experts = jnp.arange(e, dtype=I32)[:, None, None]
    dest = rank + jnp.sum(jnp.where(idx[None] == experts, pad_start[:, None, None], 0), axis=0)
    n_blocks = -(-(t * TOP_K) // MOE_BLOCK) + e
    tok = jnp.broadcast_to(jnp.arange(t, dtype=I32)[None, :], (TOP_K, t))
    row_tok = jnp.zeros((n_blocks * MOE_BLOCK,), I32).at[dest.reshape(-1)].set(tok.reshape(-1))
    nused = (pad_end[-1:] // MOE_BLOCK).astype(I32)
    block_row0 = jnp.arange(n_blocks, dtype=I32) * MOE_BLOCK
    block_exp = jnp.minimum(jnp.sum((pad_end[None, :] <= block_row0[:, None]).astype(I32), axis=1), e - 1)
    y_rows = _experts(f.reshape(t, d), row_tok.reshape(n_blocks, 1, MOE_BLOCK), block_exp, nused,
                      w_gu, b_gu, w_dn, b_dn)
    n_tiles = t // ROW_TILE
    dest_tiles = dest.reshape(TOP_K, n_tiles, ROW_TILE).transpose(1, 0, 2).reshape(n_tiles, 1, TOP_K * ROW_TILE)
    return _combine(y_rows, dest_tiles, gate.T, xall, mod, final_g, n_ctx_tiles, ctx_row, final_norm)


def kernel(x, c, ctx, c_ctx, mod_w, mod_b, norm1_g, norm2_g, final_norm_g, ssd_in_w, ssd_conv_w, ssd_conv_b,
           ssd_dt_bias, ssd_a_log, ssd_d, ssd_norm_g, ssd_out_w, attn_qkv_w, attn_q_norm_g, attn_k_norm_g,
           attn_out_w, moe_router_w, moe_router_b, moe_w_gate_up, moe_b_gate_up, moe_w_down, moe_b_down):
    bsz, seq, d = x.shape
    n_ctx = ctx.shape[1]
    depth = mod_w.shape[0]
    assert n_ctx % ROW_TILE == 0 and seq % ROW_TILE == 0 and depth == 2
    s = n_ctx + seq
    n_ctx_tiles = n_ctx // ROW_TILE
    ctx_row = bsz

    rows = -(-(bsz + 1) // SUBLANES) * SUBLANES
    c_pad = jnp.zeros((rows, d), F32).at[:bsz].set(c).at[bsz].set(c_ctx)
    mod = _modulation(c_pad, mod_w, mod_b).reshape(depth, rows, N_MOD, d)

    xall = jnp.concatenate([ctx, x], axis=1)
    upper = (jnp.arange(ROW_TILE)[:, None] < jnp.arange(ROW_TILE)[None, :]).astype(BF16)

    d_inner = ssd_out_w.shape[1]
    heads = d_inner // SSD_HEAD_DIM
    hpg = heads // SSD_GROUPS
    gn = SSD_GROUPS * SSD_STATE
    conv_ch = d_inner + 2 * gn
    in_w = ssd_in_w[0]
    w_z = in_w[:, :2 * d_inner].astype(BF16)
    w_xbc = in_w[:, 2 * d_inner:2 * d_inner + conv_ch].astype(BF16)
    w_dt = jnp.zeros((d, LANES), F32).at[:, :2 * heads].set(in_w[:, 2 * d_inner + conv_ch:]).astype(BF16)

    h = _norm_mod(xall, norm1_g[0], mod[0], n_ctx_tiles, ctx_row, shift=0, scale=1)
    h2 = h.reshape(bsz * s, d)
    z = _matmul(h2, w_z, BF16, "ssd_in_z").reshape(bsz, s, 2 * d_inner)
    xbc = _matmul(h2, w_xbc, F32, "ssd_in_xbc").reshape(bsz, s, conv_ch)
    dtraw = _matmul(h2, w_dt, F32, "ssd_in_dt").reshape(bsz, s, LANES)
    xc = _conv_silu(xbc, ssd_conv_w[0], ssd_conv_b[0], n_ctx_tiles)

    src = jnp.arange(LANES)[None, None, :, None]
    dst = jnp.arange(LANES)[None, None, None, :]
    dd = jnp.arange(2)[:, None, None, None]
    gg = jnp.arange(SSD_GROUPS)[None, :, None, None]
    sel = jnp.logical_and(src == dd * heads + gg * hpg + dst, dst < hpg).astype(BF16).reshape(2 * SSD_GROUPS, LANES, LANES)

    def per_group(p):
        return jnp.zeros((2, SSD_GROUPS, LANES), F32).at[:, :, :hpg].set(
            p.reshape(2, SSD_GROUPS, hpg)).reshape(2 * SSD_GROUPS, 1, LANES)

    bias_g = per_group(ssd_dt_bias[0])
    a_g = per_group(-jnp.exp(ssd_a_log[0].astype(F32)))
    tri_f = (jnp.arange(SSD_CHUNK)[:, None] >= jnp.arange(SSD_CHUNK)[None, :]).astype(BF16)
    n_ctx_chunks = n_ctx // SSD_CHUNK
    yf = _ssd_scan(xc, dtraw, sel, bias_g, a_g, tri_f, n_ctx_chunks, False, d_inner)
    yb = _ssd_scan(xc, dtraw, sel, bias_g, a_g, tri_f.T, n_ctx_chunks, True, d_inner)
    dskip = jnp.repeat(ssd_d[0], SSD_HEAD_DIM, axis=1)
    xall = _ssd_out(yf, yb, z, xc, dskip, ssd_norm_g[0], ssd_out_w[0].astype(BF16), xall, mod[0],
                    n_ctx_tiles, ctx_row)
    xall = _moe(xall, norm2_g[0], mod[0], moe_router_w[0], moe_router_b[0], moe_w_gate_up[0].astype(BF16),
                moe_b_gate_up[0], moe_w_down[0].astype(BF16), moe_b_down[0], upper, final_norm_g,
                n_ctx_tiles, ctx_row, False)

    n_q = attn_out_w.shape[1]
    n_kv = ATTN_KV_HEADS * ATTN_HEAD_DIM
    pairs = ATTN_HEAD_DIM // 4
    pos = jnp.arange(seq, dtype=I32)
    inv_freq = jnp.power(ROPE_THETA, -jnp.arange(pairs, dtype=F32) / pairs)
    ang = jnp.concatenate([(pos // GRID_W).astype(F32)[:, None] * inv_freq,
                           (pos % GRID_W).astype(F32)[:, None] * inv_freq], axis=-1)
    ang = jnp.concatenate([jnp.zeros((n_ctx, ATTN_HEAD_DIM // 2), F32), ang], axis=0)
    cos_h, sin_h = jnp.cos(ang), jnp.sin(ang)
    cos_t = jnp.tile(jnp.concatenate([cos_h, cos_h], axis=-1), (1, LANES // ATTN_HEAD_DIM))[None]
    sin_t = jnp.tile(jnp.concatenate([-sin_h, sin_h], axis=-1), (1, LANES // ATTN_HEAD_DIM))[None]
    blk = 2 * LANES
    ones_blk = (jnp.arange(blk)[:, None] // ATTN_HEAD_DIM == jnp.arange(blk)[None, :] // ATTN_HEAD_DIM).astype(BF16)
    qg = jnp.tile(attn_q_norm_g[0], blk // ATTN_HEAD_DIM)[None]
    kg = jnp.tile(attn_k_norm_g[0], blk // ATTN_HEAD_DIM)[None]

    h = _norm_mod(xall, norm1_g[1], mod[1], n_ctx_tiles, ctx_row, shift=0, scale=1)
    q, k, vt = _qkv(h, attn_qkv_w[0].astype(BF16), ones_blk, qg, kg, cos_t, sin_t, n_q, n_kv)
    o = _flash(q, k, vt, n_ctx)
    lat = _attn_out(o, attn_out_w[0].astype(BF16), xall, mod[1], n_ctx_tiles)
    return _moe(lat, norm2_g[1], mod[1], moe_router_w[1], moe_router_b[1], moe_w_gate_up[1].astype(BF16),
                moe_b_gate_up[1], moe_w_down[1].astype(BF16), moe_b_down[1], upper, final_norm_g,
                0, ctx_row, True)
```
